```python
import math
import jax, jax.numpy as jnp
from jax import lax
import numpy as np

D_MODEL = 1024
BATCH = 8
SEQ = 4096
DEPTH = 1

N_META = 16
ATT_HEADS = 4
ATT_HEAD_DIM = 64
ATT_V_DIM = 2 * ATT_HEAD_DIM
REC_HEADS = 4
REC_DK = 128
REC_DV = 128
ATT_WIDTH = ATT_HEADS * ATT_V_DIM
REC_WIDTH = REC_HEADS * REC_DV
MIX_WIDTH = ATT_WIDTH + REC_WIDTH
COL_SIZES = (
    ATT_HEADS * 2 * ATT_HEAD_DIM,
    ATT_HEADS * 2 * ATT_HEAD_DIM,
    ATT_WIDTH,
    REC_HEADS * REC_DK,
    REC_HEADS * REC_DK,
    REC_WIDTH,
    REC_WIDTH,
)
IN_COLS = sum(COL_SIZES)
SPLIT_POINTS = tuple(int(v) for v in np.cumsum(COL_SIZES)[:-1])
Q_BLOCK = 128
REC_CHUNK = 64
ROPE_THETA = 10000.0
PEER_HEADS = 8
PEER_QUERY_DIM = 256
N_KEYS = 128
N_EXPERTS = N_KEYS * N_KEYS
PEER_TOPK = 16
PEER_BLOCK = 128
EPS = 1e-6

kernel_name = "hymba_diffattn_hgrn2_peer"


def rmsnorm(x, w):
    xf = x.astype(jnp.float32)
    y = xf * lax.rsqrt(jnp.mean(xf * xf, axis=-1, keepdims=True) + EPS)
    return (y * w.astype(jnp.float32)).astype(x.dtype)


def rope_tables(T, d):
    inv_freq = ROPE_THETA ** (-jnp.arange(0, d, 2, dtype=jnp.float32) / d)
    ang = jnp.arange(T, dtype=jnp.float32)[:, None] * inv_freq[None, :]
    ang = jnp.concatenate([ang, ang], axis=-1)
    return jnp.cos(ang), jnp.sin(ang)


def apply_rope(z, cos, sin):
    half = z.shape[-1] // 2
    rot = jnp.concatenate([-z[..., half:], z[..., :half]], axis=-1)
    return (z * cos.astype(z.dtype) + rot * sin.astype(z.dtype)).astype(z.dtype)


def lambda_init(layer_idx):
    return 0.8 - 0.6 * math.exp(-0.3 * layer_idx)


def diff_attention(aq, ak, av, lq1, lk1, lq2, lk2, subln_w, lam_init, cos, sin):
    B, T, _ = aq.shape
    q = apply_rope(aq.reshape(B, T, ATT_HEADS, 2, ATT_HEAD_DIM).transpose(3, 0, 2, 1, 4), cos, sin)
    k = apply_rope(ak.reshape(B, T, ATT_HEADS, 2, ATT_HEAD_DIM).transpose(3, 0, 2, 1, 4), cos, sin)
    v = av.reshape(B, T, ATT_HEADS, ATT_V_DIM).transpose(0, 2, 1, 3)
    q = q * (ATT_HEAD_DIM ** -0.5)
    lam = (jnp.exp(jnp.sum(lq1.astype(jnp.float32) * lk1.astype(jnp.float32)))
           - jnp.exp(jnp.sum(lq2.astype(jnp.float32) * lk2.astype(jnp.float32))) + lam_init)
    neg = jnp.finfo(jnp.float32).min

    def block(q_blk, k_blk, v_blk, q_start):
        nq, nk = q_blk.shape[-2], k_blk.shape[-2]
        mask = jnp.arange(nk)[None, :] <= (q_start + jnp.arange(nq))[:, None]
        s = jnp.einsum('pbhqd,pbhkd->pbhqk', q_blk, k_blk).astype(jnp.float32)
        p = jax.nn.softmax(jnp.where(mask, s, neg), axis=-1)
        w = p[0] - lam * p[1]
        return jnp.einsum('bhqk,bhkv->bhqv', w.astype(v_blk.dtype), v_blk)

    outs = [block(q[..., :N_META, :], k[..., :N_META, :], v[:, :, :N_META], 0)]
    for i in range((T - N_META) // Q_BLOCK):
        s0 = N_META + i * Q_BLOCK
        e0 = s0 + Q_BLOCK
        outs.append(block(q[..., s0:e0, :], k[..., :e0, :], v[:, :, :e0], s0))
    o = jnp.concatenate(outs, axis=2)
    o = rmsnorm(o, subln_w) * (1.0 - lam_init)
    return o.transpose(0, 2, 1, 3).reshape(B, T, ATT_WIDTH)


def gla_chunk(S, q, k, v, g):
    C = q.shape[2]
    cum = jnp.cumsum(g, axis=2)
    inter = jnp.einsum('bhtc,bhcv->bhtv', q * jnp.exp(cum), S)
    causal = jnp.tril(jnp.ones((C, C), dtype=bool))[:, :, None]
    diff = cum[:, :, :, None, :] - cum[:, :, None, :, :]
    decay = jnp.where(causal, jnp.exp(jnp.where(causal, diff, 0.0)), 0.0)
    A = jnp.einsum('bhtc,bhsc,bhtsc->bhts', q, k, decay)
    intra = jnp.einsum('bhts,bhsv->bhtv', A, v)
    last = cum[:, :, -1]
    S_new = (jnp.exp(last)[..., None] * S
             + jnp.einsum('bhsc,bhsv->bhcv', k * jnp.exp(last[:, :, None, :] - cum), v))
    return S_new, inter + intra


def hgrn2(rq, rf, ri, rg, lb, norm_w):
    B, T, _ = rq.shape

    def heads(z, d):
        return z.reshape(B, T, REC_HEADS, d).transpose(0, 2, 1, 3).astype(jnp.float32)

    q = jax.nn.silu(heads(rq, REC_DK))
    lbh = lb.reshape(REC_HEADS, 1, REC_DK)
    f = lbh + (1.0 - lbh) * jax.nn.sigmoid(heads(rf, REC_DK))
    k = 1.0 - f
    g = jnp.log(f)
    v = heads(ri, REC_DV)
    S0 = jnp.zeros((B, REC_HEADS, REC_DK, REC_DV), jnp.float32)
    S_meta, o_meta = gla_chunk(S0, q[:, :, :N_META], k[:, :, :N_META], v[:, :, :N_META], g[:, :, :N_META])
    n_chunks = (T - N_META) // REC_CHUNK

    def to_chunks(z):
        z = z[:, :, N_META:].reshape(B, REC_HEADS, n_chunks, REC_CHUNK, z.shape[-1])
        return z.transpose(2, 0, 1, 3, 4)

    _, o_real = lax.scan(lambda S, xs: gla_chunk(S, *xs), S_meta,
                         (to_chunks(q), to_chunks(k), to_chunks(v), to_chunks(g)))
    o_real = o_real.transpose(1, 2, 0, 3, 4).reshape(B, REC_HEADS, T - N_META, REC_DV)
    o = jnp.concatenate([o_meta, o_real], axis=2)
    o = rmsnorm(o, norm_w[:, None, :])
    o = o.transpose(0, 2, 1, 3).reshape(B, T, REC_WIDTH) * jax.nn.silu(rg.astype(jnp.float32))
    return o.astype(rq.dtype)


def token_mix(a, w_in, lb, rec_norm_w, lq1, lk1, lq2, lk2, subln_w, w_out, lam_init, cos, sin):
    proj = jnp.einsum('btd,dc->btc', a, w_in)
    aq, ak, av, rq, rf, ri, rg = jnp.split(proj, SPLIT_POINTS, axis=-1)
    att = diff_attention(aq, ak, av, lq1, lk1, lq2, lk2, subln_w, lam_init, cos, sin)
    rec = hgrn2(rq, rf, ri, rg, lb, rec_norm_w)
    mixed = jnp.concatenate([att, rec.astype(att.dtype)], axis=-1)
    return jnp.einsum('btc,cd->btd', mixed, w_out)


def peer_ffn(h, w_query, subkeys, u_table, v_table):
    B, T, D = h.shape
    n = B * T
    pad = (-n) % PEER_BLOCK
    blocks = jnp.pad(h.reshape(n, D), ((0, pad), (0, 0))).reshape(-1, PEER_BLOCK, D)

    def one_block(xb):
        q = jnp.einsum('nd,dc->nc', xb, w_query).reshape(PEER_BLOCK, PEER_HEADS, 2, PEER_QUERY_DIM // 2)
        s = jnp.einsum('nhpc,hpkc->nhpk', q, subkeys).astype(jnp.float32)
        sv, si = lax.top_k(s, PEER_TOPK)
        cand_s = (sv[:, :, 0, :, None] + sv[:, :, 1, None, :]).reshape(PEER_BLOCK, PEER_HEADS, PEER_TOPK * PEER_TOPK)
        cand_i = (si[:, :, 0, :, None] * N_KEYS + si[:, :, 1, None, :]).reshape(PEER_BLOCK, PEER_HEADS, PEER_TOPK * PEER_TOPK)
        top_s, pos = lax.top_k(cand_s, PEER_TOPK)
        experts = jnp.take_along_axis(cand_i, pos, axis=-1).reshape(PEER_BLOCK, PEER_HEADS * PEER_TOPK)
        gates = jax.nn.softmax(top_s, axis=-1).reshape(PEER_BLOCK, PEER_HEADS * PEER_TOPK)
        u = u_table[experts]
        hid = jax.nn.gelu(jnp.einsum('nkd,nd->nk', u, xb).astype(jnp.float32), approximate=False) * gates
        return jnp.einsum('nk,nkd->nd', hid.astype(xb.dtype), v_table[experts])

    y = lax.map(one_block, blocks).reshape(-1, D)[:n]
    return y.reshape(B, T, D)


def setup_inputs(seed: int = 0) -> dict:
    key = jax.random.key(seed)
    ks = jax.random.split(key, 18)
    nrm = jax.random.normal
    f32 = jnp.float32
    return {
        "x": nrm(ks[0], (BATCH, SEQ, D_MODEL), f32),
        "meta_tokens": nrm(ks[1], (N_META, D_MODEL), f32),
        "mix_norm_w": 1.0 + 0.02 * nrm(ks[2], (DEPTH, D_MODEL), f32),
        "w_in": nrm(ks[3], (DEPTH, D_MODEL, IN_COLS), f32) * D_MODEL ** -0.5,
        "rec_lb_logits": 0.5 * nrm(ks[4], (DEPTH + 1, REC_HEADS * REC_DK), f32),
        "rec_norm_w": 1.0 + 0.02 * nrm(ks[5], (DEPTH, REC_HEADS, REC_DV), f32),
        "diff_lambda_q1": 0.1 * nrm(ks[6], (DEPTH, ATT_HEAD_DIM), f32),
        "diff_lambda_k1": 0.1 * nrm(ks[7], (DEPTH, ATT_HEAD_DIM), f32),
        "diff_lambda_q2": 0.1 * nrm(ks[8], (DEPTH, ATT_HEAD_DIM), f32),
        "diff_lambda_k2": 0.1 * nrm(ks[9], (DEPTH, ATT_HEAD_DIM), f32),
        "diff_subln_w": 1.0 + 0.02 * nrm(ks[10], (DEPTH, ATT_V_DIM), f32),
        "w_out": nrm(ks[11], (DEPTH, MIX_WIDTH, D_MODEL), f32) * MIX_WIDTH ** -0.5,
        "ffn_norm_w": 1.0 + 0.02 * nrm(ks[12], (DEPTH, D_MODEL), f32),
        "peer_w_query": nrm(ks[13], (DEPTH, D_MODEL, PEER_HEADS * PEER_QUERY_DIM), f32) * D_MODEL ** -0.5,
        "peer_subkeys": nrm(ks[14], (DEPTH, PEER_HEADS, 2, N_KEYS, PEER_QUERY_DIM // 2), f32) * (PEER_QUERY_DIM // 2) ** -0.5,
        "peer_u": nrm(ks[15], (DEPTH, N_EXPERTS, D_MODEL), f32) * D_MODEL ** -0.5,
        "peer_v": nrm(ks[16], (DEPTH, N_EXPERTS, D_MODEL), f32) * PEER_HEADS ** -0.5,
        "final_norm_w": 1.0 + 0.02 * nrm(ks[17], (D_MODEL,), f32),
    }


def reference(x, meta_tokens, mix_norm_w, w_in, rec_lb_logits, rec_norm_w,
              diff_lambda_q1, diff_lambda_k1, diff_lambda_q2, diff_lambda_k2,
              diff_subln_w, w_out, ffn_norm_w, peer_w_query, peer_subkeys,
              peer_u, peer_v, final_norm_w):
    B = x.shape[0]
    T = N_META + x.shape[1]
    meta = jnp.broadcast_to(meta_tokens[None].astype(x.dtype), (B, N_META, D_MODEL))
    h = jnp.concatenate([meta, x], axis=1)
    cos, sin = rope_tables(T, ATT_HEAD_DIM)
    lb_all = jnp.cumsum(jax.nn.softmax(rec_lb_logits.astype(jnp.float32), axis=0), axis=0)
    for l in range(DEPTH):
        a = rmsnorm(h, mix_norm_w[l])
        h = h + token_mix(a, w_in[l], lb_all[l], rec_norm_w[l],
                          diff_lambda_q1[l], diff_lambda_k1[l], diff_lambda_q2[l], diff_lambda_k2[l],
                          diff_subln_w[l], w_out[l], lambda_init(l), cos, sin).astype(h.dtype)
        if l == DEPTH - 1:
            h = h[:, N_META:]
        h = h + peer_ffn(rmsnorm(h, ffn_norm_w[l]), peer_w_query[l], peer_subkeys[l],
                         peer_u[l], peer_v[l]).astype(h.dtype)
    return rmsnorm(h, final_norm_w)
```

```python
import functools
import math

import numpy as np
import jax
import jax.numpy as jnp
from jax import lax
from jax.experimental import pallas as pl
from jax.experimental.pallas import tpu as pltpu

F32 = jnp.float32
BF16 = jnp.bfloat16

D_MODEL = 1024
N_META = 16
ATT_HEADS = 4
ATT_HEAD_DIM = 64
REC_HEADS = 4
HEAD_W = 128
GROUP_W = 512
N_GROUPS = 7
ROPE_THETA = 10000.0
PEER_HEADS = 8
N_KEYS = 128
N_EXPERTS = N_KEYS * N_KEYS
PEER_TOPK = 16
EPS = 1e-6
LAM_INIT = 0.8 - 0.6 * math.exp(-0.3 * 0)
NEG = -1e30
REC_CHUNK = 128
VMEM_LIMIT = 56 * 1024 * 1024

_NT = (((1,), (1,)), ((), ()))


def _rms(x, w):
    return x * lax.rsqrt(jnp.mean(x * x, axis=-1, keepdims=True) + EPS) * w


def _inproj_kernel(x_ref, nw_ref, w_ref, cos_ref, sin_ref,
                   q_ref, k_ref, v_ref, rq_ref, rf_ref, ri_ref, rg_ref):
    a = _rms(x_ref[...], nw_ref[...]).astype(BF16)
    cos = cos_ref[...]
    sin = sin_ref[...]
    lane = lax.broadcasted_iota(jnp.int32, (1, HEAD_W), 1)
    lo = (lane % ATT_HEAD_DIM) < (ATT_HEAD_DIM // 2)

    def proj(c):
        return jnp.dot(a, w_ref[:, c * GROUP_W:(c + 1) * GROUP_W],
                       preferred_element_type=F32)

    def rope(y, scale):
        outs = []
        for h in range(ATT_HEADS):
            z = y[:, h * HEAD_W:(h + 1) * HEAD_W]
            rot = jnp.where(lo, pltpu.roll(z, HEAD_W - 32, 1), pltpu.roll(z, 32, 1))
            r = z * cos + rot * sin
            if scale is not None:
                r = r * scale
            outs.append(r)
        return jnp.concatenate(outs, axis=1)

    q_ref[...] = rope(proj(0), ATT_HEAD_DIM ** -0.5).astype(q_ref.dtype)
    k_ref[...] = rope(proj(1), None).astype(k_ref.dtype)
    v_ref[...] = proj(2).astype(v_ref.dtype)
    rq_ref[...] = proj(3).astype(rq_ref.dtype)
    rf_ref[...] = proj(4).astype(rf_ref.dtype)
    ri_ref[...] = proj(5).astype(ri_ref.dtype)
    rg_ref[...] = proj(6).astype(rg_ref.dtype)


def _inproj(x2, nw, w_bf, cos_t, sin_t, tm):
    n = x2.shape[0]
    n_pos = cos_t.shape[0] // tm
    row = lambda i: (i, 0)
    fixed = lambda i: (0, 0)
    pos = lambda i: (i % n_pos, 0)
    dts = (BF16, BF16, BF16, BF16, F32, BF16, BF16)
    return pl.pallas_call(
        _inproj_kernel,
        grid=(n // tm,),
        in_specs=[pl.BlockSpec((tm, D_MODEL), row),
                  pl.BlockSpec((1, D_MODEL), fixed),
                  pl.BlockSpec((D_MODEL, N_GROUPS * GROUP_W), fixed),
                  pl.BlockSpec((tm, HEAD_W), pos),
                  pl.BlockSpec((tm, HEAD_W), pos)],
        out_specs=[pl.BlockSpec((tm, GROUP_W), row)] * N_GROUPS,
        out_shape=[jax.ShapeDtypeStruct((n, GROUP_W), dt) for dt in dts],
        compiler_params=pltpu.CompilerParams(
            dimension_semantics=("arbitrary",), vmem_limit_bytes=VMEM_LIMIT),
        name="inproj",
    )(x2, nw, w_bf, cos_t, sin_t)


def _attn_kernel(lam_ref, q_ref, k_ref, v_ref, km_ref, vm_ref, sw_ref, o_ref,
                 m_scr, l_scr, acc_scr, *, bq):
    qi = pl.program_id(2)
    q = q_ref[...]
    lane = lax.broadcasted_iota(jnp.int32, (1, HEAD_W), 1)
    zero = jnp.zeros_like(q)
    qq = jnp.concatenate([jnp.where(lane < ATT_HEAD_DIM, q, zero),
                          jnp.where(lane >= ATT_HEAD_DIM, q, zero)], axis=0)

    m_scr[...] = jnp.full(m_scr.shape, NEG, F32)
    l_scr[...] = jnp.zeros(l_scr.shape, F32)
    acc_scr[...] = jnp.zeros(acc_scr.shape, F32)

    def step(kb, vb, mask):
        s = lax.dot_general(qq, kb, _NT, preferred_element_type=F32)
        if mask is not None:
            s = jnp.where(mask, s, NEG)
        m_prev = m_scr[...]
        m_new = jnp.maximum(m_prev, jnp.max(s, axis=1, keepdims=True))
        alpha = jnp.exp(m_prev - m_new)
        p = jnp.exp(s - m_new)
        l_scr[...] = alpha * l_scr[...] + jnp.sum(p, axis=1, keepdims=True)
        acc_scr[...] = alpha * acc_scr[...] + jnp.dot(
            p.astype(BF16), vb, preferred_element_type=F32)
        m_scr[...] = m_new

    col_m = lax.broadcasted_iota(jnp.int32, (1, km_ref.shape[0]), 1)
    step(km_ref[...], vm_ref[...], col_m < N_META)

    def body(j, carry):
        start = pl.multiple_of(j * bq, bq)
        step(k_ref[pl.ds(start, bq), :], v_ref[pl.ds(start, bq), :], None)
        return carry

    lax.fori_loop(0, qi, body, 0)

    start = pl.multiple_of(qi * bq, bq)
    row = lax.broadcasted_iota(jnp.int32, (2 * bq, bq), 0) % bq
    col = lax.broadcasted_iota(jnp.int32, (2 * bq, bq), 1)
    step(k_ref[pl.ds(start, bq), :], v_ref[pl.ds(start, bq), :], col <= row)

    lv = lam_ref[...]
    lam = (jnp.exp(jnp.sum(lv[0:1] * lv[1:2], axis=1, keepdims=True))
           - jnp.exp(jnp.sum(lv[2:3] * lv[3:4], axis=1, keepdims=True)) + LAM_INIT)
    acc = acc_scr[...]
    l = l_scr[...]
    o = acc[:bq] / l[:bq] - lam * (acc[bq:] / l[bq:])
    o_ref[...] = (_rms(o, sw_ref[...]) * (1.0 - LAM_INIT)).astype(o_ref.dtype)


def _attention(lam4, q, k, v, km, vm, sw, bq):
    B, S, _ = q.shape
    kernel = functools.partial(_attn_kernel, bq=bq)
    return pl.pallas_call(
        kernel,
        grid=(B, ATT_HEADS, S // bq),
        in_specs=[pl.BlockSpec((4, ATT_HEAD_DIM), lambda b, h, i: (0, 0)),
                  pl.BlockSpec((None, bq, HEAD_W), lambda b, h, i: (b, i, h)),
                  pl.BlockSpec((None, S, HEAD_W), lambda b, h, i: (b, 0, h)),
                  pl.BlockSpec((None, S, HEAD_W), lambda b, h, i: (b, 0, h)),
                  pl.BlockSpec((km.shape[0], HEAD_W), lambda b, h, i: (0, h)),
                  pl.BlockSpec((km.shape[0], HEAD_W), lambda b, h, i: (0, h)),
                  pl.BlockSpec((1, HEAD_W), lambda b, h, i: (0, 0))],
        out_specs=pl.BlockSpec((None, bq, HEAD_W), lambda b, h, i: (b, i, h)),
        out_shape=jax.ShapeDtypeStruct((B, S, GROUP_W), BF16),
        scratch_shapes=[pltpu.VMEM((2 * bq, 1), F32),
                        pltpu.VMEM((2 * bq, 1), F32),
                        pltpu.VMEM((2 * bq, HEAD_W), F32)],
        compiler_params=pltpu.CompilerParams(
            dimension_semantics=("arbitrary", "arbitrary", "arbitrary"),
            vmem_limit_bytes=VMEM_LIMIT),
        name="diff_attention",
    )(lam4, q, k, v, km, vm, sw)


def _cum_matrix():
    c = REC_CHUNK
    r = np.arange(c)
    tri = (r[None, :] <= r[:, None]).astype(np.float32)
    mats = [tri, tri[16 * (r // 16) + 7]]
    for half in (16, 32, 64):
        mats.append(tri[(r // (2 * half)) * (2 * half) + half - 1])
    return np.concatenate(mats, axis=0)


def _gates(rq, rf, lb):
    q = rq * jax.nn.sigmoid(rq)
    f = lb + (1.0 - lb) * jax.nn.sigmoid(rf)
    return q, 1.0 - f, jnp.log(f)


def _hgrn_kernel(cm_ref, lb_ref, nw_ref, rq_ref, rf_ref, ri_ref, rg_ref,
                 mq_ref, mf_ref, mi_ref, o_ref, st_scr):
    c_idx = pl.program_id(1)
    C = REC_CHUNK
    cm = cm_ref[...]

    def cumsums(g):
        return jnp.dot(cm, g, preferred_element_type=F32,
                       precision=lax.Precision.HIGHEST)

    def state_update(st, k, v32, cum):
        last = cum[C - 1:C]
        ke = (k * jnp.exp(last - cum)).astype(BF16)
        upd = jnp.dot(v32.T.astype(BF16), ke, preferred_element_type=F32)
        return st * jnp.exp(last) + upd

    @pl.when(c_idx == 0)
    def _():
        valid = lax.broadcasted_iota(jnp.int32, (C, 1), 0) >= C - N_META
        for h in range(REC_HEADS):
            sl = slice(h * HEAD_W, (h + 1) * HEAD_W)
            _, k, g = _gates(mq_ref[:, sl].astype(F32), mf_ref[:, sl], lb_ref[:, sl])
            k = jnp.where(valid, k, 0.0)
            g = jnp.where(valid, g, 0.0)
            v32 = jnp.where(valid, mi_ref[:, sl].astype(F32), 0.0)
            cum = cumsums(g)[:C]
            st_scr[h] = state_update(jnp.zeros((HEAD_W, HEAD_W), F32), k, v32, cum)

    t = lax.broadcasted_iota(jnp.int32, (C, C), 0)
    s = lax.broadcasted_iota(jnp.int32, (C, C), 1)
    mask_d = (t // 16 == s // 16) & (s <= t)
    masks = [(t // (2 * hf) == s // (2 * hf)) & ((t // hf) % 2 == 1) & ((s // hf) % 2 == 0)
             for hf in (16, 32, 64)]

    for h in range(REC_HEADS):
        sl = slice(h * HEAD_W, (h + 1) * HEAD_W)
        q, k, g = _gates(rq_ref[:, sl].astype(F32), rf_ref[:, sl], lb_ref[:, sl])
        v32 = ri_ref[:, sl].astype(F32)
        call = cumsums(g)
        cum = call[:C]
        st = st_scr[h]
        qe = (q * jnp.exp(cum)).astype(BF16)
        o = lax.dot_general(qe, st.astype(BF16), _NT, preferred_element_type=F32)

        def pair_scores(ref, clamp):
            dq = cum - ref
            dk = ref - cum
            if clamp:
                dq = jnp.minimum(dq, 0.0)
                dk = jnp.minimum(dk, 0.0)
            return lax.dot_general((q * jnp.exp(dq)).astype(BF16),
                                   (k * jnp.exp(dk)).astype(BF16), _NT,
                                   preferred_element_type=F32)

        a = jnp.where(mask_d, pair_scores(call[C:2 * C], False), 0.0)
        for lvl, mk in enumerate(masks):
            a = jnp.where(mk, pair_scores(call[(2 + lvl) * C:(3 + lvl) * C], True), a)
        o = o + jnp.dot(a.astype(BF16), v32.astype(BF16), preferred_element_type=F32)
        st_scr[h] = state_update(st, k, v32, cum)

        rg = rg_ref[:, sl].astype(F32)
        o = _rms(o, nw_ref[:, sl]) * (rg * jax.nn.sigmoid(rg))
        o_ref[:, sl] = o.astype(o_ref.dtype)


def _hgrn(cm, lb, nw, rq, rf, ri, rg, mq, mf, mi):
    B, S, _ = rq.shape
    C = REC_CHUNK
    tok = pl.BlockSpec((None, C, GROUP_W), lambda b, c: (b, c, 0))
    fix = lambda shape: pl.BlockSpec(shape, lambda b, c: (0, 0))
    return pl.pallas_call(
        _hgrn_kernel,
        grid=(B, S // C),
        in_specs=[fix(cm.shape), fix((1, GROUP_W)), fix((1, GROUP_W)),
                  tok, tok, tok, tok,
                  fix((C, GROUP_W)), fix((C, GROUP_W)), fix((C, GROUP_W))],
        out_specs=tok,
        out_shape=jax.ShapeDtypeStruct((B, S, GROUP_W), BF16),
        scratch_shapes=[pltpu.VMEM((REC_HEADS, HEAD_W, HEAD_W), F32)],
        compiler_params=pltpu.CompilerParams(
            dimension_semantics=("arbitrary", "arbitrary"),
            vmem_limit_bytes=VMEM_LIMIT),
        name="hgrn2",
    )(cm, lb, nw, rq, rf, ri, rg, mq, mf, mi)


def _outproj_kernel(x_ref, att_ref, rec_ref, wo_ref, fw_ref, wq_ref, sk_ref,
                    h_ref, xt_ref, st_ref):
    h1 = (x_ref[...]
          + jnp.dot(att_ref[...], wo_ref[:GROUP_W, :], preferred_element_type=F32)
          + jnp.dot(rec_ref[...], wo_ref[GROUP_W:, :], preferred_element_type=F32))
    h_ref[...] = h1
    xt = _rms(h1, fw_ref[...]).T.astype(BF16)
    xt_ref[...] = xt
    qt = jnp.dot(wq_ref[...], xt, preferred_element_type=F32).astype(BF16)
    for g in range(2 * PEER_HEADS):
        sl = slice(g * N_KEYS, (g + 1) * N_KEYS)
        st_ref[sl, :] = jnp.dot(sk_ref[g], qt[sl, :], preferred_element_type=F32)


def _outproj(x2, att, rec, wo, fw, wqt, sk, tm):
    n = x2.shape[0]
    row = lambda i: (i, 0)
    colb = lambda i: (0, i)
    fixed = lambda i: (0, 0)
    return pl.pallas_call(
        _outproj_kernel,
        grid=(n // tm,),
        in_specs=[pl.BlockSpec((tm, D_MODEL), row),
                  pl.BlockSpec((tm, GROUP_W), row),
                  pl.BlockSpec((tm, GROUP_W), row),
                  pl.BlockSpec((D_MODEL, D_MODEL), fixed),
                  pl.BlockSpec((1, D_MODEL), fixed),
                  pl.BlockSpec(wqt.shape, fixed),
                  pl.BlockSpec(sk.shape, lambda i: (0, 0, 0))],
        out_specs=[pl.BlockSpec((tm, D_MODEL), row),
                   pl.BlockSpec((D_MODEL, tm), colb),
                   pl.BlockSpec((wqt.shape[0], tm), colb)],
        out_shape=[jax.ShapeDtypeStruct((n, D_MODEL), F32),
                   jax.ShapeDtypeStruct((D_MODEL, n), BF16),
                   jax.ShapeDtypeStruct((wqt.shape[0], n), F32)],
        compiler_params=pltpu.CompilerParams(
            dimension_semantics=("arbitrary",), vmem_limit_bytes=VMEM_LIMIT),
        name="outproj_scores",
    )(x2, att, rec, wo, fw, wqt, sk)


def _staircase():
    return [(p, q) for p in range(PEER_TOPK) for q in range(PEER_TOPK)
            if (p + 1) * (q + 1) <= PEER_TOPK]


def _route_kernel(st_ref, lam_ref, pw_ref, r2_ref, qw_ref,
                  rank_scr, top_scr, cand_scr):
    K = PEER_TOPK
    tb = st_ref.shape[1]
    pairs = _staircase()

    for g in range(2 * PEER_HEADS):
        h, part = divmod(g, 2)
        s = st_ref[g * N_KEYS:(g + 1) * N_KEYS, :]

        def extract(p, carry):
            cur, rank = carry
            m = jnp.max(cur, axis=0, keepdims=True)
            hit = cur == m
            top_scr[part, p, h:h + 1, :] = m
            return jnp.where(hit, NEG, cur), jnp.where(hit, p.astype(F32), rank)

        _, rank = lax.fori_loop(0, K, extract, (s, jnp.full(s.shape, float(K), F32)))
        rank_scr[g] = rank

    a = [top_scr[0, p] for p in range(K)]
    b = [top_scr[1, p] for p in range(K)]
    sums = [a[p] + b[q] for (p, q) in pairs]
    for idx, v in enumerate(sums):
        cand_scr[idx] = v

    def kth(i, thr):
        cur = cand_scr[...]
        m = jnp.max(cur, axis=0)
        cand_scr[...] = jnp.where(cur == m[None], NEG, cur)
        return m

    thr = lax.fori_loop(0, K, kth, jnp.zeros((PEER_HEADS, tb), F32))
    m0 = sums[0]
    z = jnp.zeros((PEER_HEADS, tb), F32)
    lam = [jnp.zeros((PEER_HEADS, tb), F32) for _ in range(K)]
    for (p, q), v in zip(pairs, sums):
        sel = v >= thr
        z = z + jnp.where(sel, jnp.exp(v - m0), 0.0)
        lam[p] = lam[p] + jnp.where(sel, 1.0, 0.0)
    inv_z = 1.0 / z

    for h in range(PEER_HEADS):
        r1 = rank_scr[2 * h]
        lam_k = jnp.zeros((N_KEYS, tb), F32)
        for p in range(K):
            lam_k = jnp.where(r1 == float(p), lam[p][h:h + 1, :], lam_k)
        lam_ref[h] = lam_k
        s1 = st_ref[(2 * h) * N_KEYS:(2 * h + 1) * N_KEYS, :]
        pw_ref[h] = jnp.exp(s1 - a[0][h:h + 1, :]) * inv_z[h:h + 1, :]
        s2 = st_ref[(2 * h + 1) * N_KEYS:(2 * h + 2) * N_KEYS, :]
        qw_ref[h] = jnp.exp(s2 - b[0][h:h + 1, :]).astype(qw_ref.dtype)
        r2_ref[h] = rank_scr[2 * h + 1].astype(r2_ref.dtype)


def _route(st, tb):
    n = st.shape[1]
    n_pairs = len(_staircase())
    spec = pl.BlockSpec((PEER_HEADS, N_KEYS, tb), lambda i: (0, 0, i))
    shp = lambda dt: jax.ShapeDtypeStruct((PEER_HEADS, N_KEYS, n), dt)
    return pl.pallas_call(
        _route_kernel,
        grid=(n // tb,),
        in_specs=[pl.BlockSpec((st.shape[0], tb), lambda i: (0, i))],
        out_specs=[spec, spec, spec, spec],
        out_shape=[shp(F32), shp(F32), shp(BF16), shp(BF16)],
        scratch_shapes=[pltpu.VMEM((2 * PEER_HEADS, N_KEYS, tb), F32),
                        pltpu.VMEM((2, PEER_TOPK, PEER_HEADS, tb), F32),
                        pltpu.VMEM((n_pairs, PEER_HEADS, tb), F32)],
        compiler_params=pltpu.CompilerParams(
            dimension_semantics=("arbitrary",), vmem_limit_bytes=VMEM_LIMIT),
        name="peer_route",
    )(st)


def _peer_kernel(xt_ref, u_ref, vt_ref, lam_ref, pw_ref, r2_ref, qw_ref,
                 h_ref, fw_ref, o_ref, acc_scr, *, te):
    e = pl.program_id(1)
    tb = xt_ref.shape[1]

    @pl.when(e == 0)
    def _():
        acc_scr[...] = jnp.zeros(acc_scr.shape, F32)

    ht = jnp.dot(u_ref[...], xt_ref[...], preferred_element_type=F32)
    act = 0.5 * ht * (1.0 + lax.erf(ht * (2.0 ** -0.5)))
    hid = []
    for ii in range(te // N_KEYS):
        i = e * (te // N_KEYS) + ii
        gate = jnp.zeros((N_KEYS, tb), BF16)
        for h in range(PEER_HEADS):
            lam_b = jnp.broadcast_to(lam_ref[h, pl.ds(i, 1), :], (N_KEYS, tb)).astype(BF16)
            p_b = jnp.broadcast_to(pw_ref[h, pl.ds(i, 1), :], (N_KEYS, tb)).astype(BF16)
            gate = gate + jnp.where(r2_ref[h] < lam_b, qw_ref[h], jnp.zeros_like(p_b)) * p_b
        hid.append((act[ii * N_KEYS:(ii + 1) * N_KEYS] * gate.astype(F32)).astype(BF16))
    hid = jnp.concatenate(hid, axis=0) if len(hid) > 1 else hid[0]
    acc_scr[...] += jnp.dot(vt_ref[...], hid, preferred_element_type=F32)

    @pl.when(e == pl.num_programs(1) - 1)
    def _():
        o_ref[...] = _rms(h_ref[...] + acc_scr[...].T, fw_ref[...])


def _peer(xt, u_bf, vt_bf, lam, pw, r2, qw, h1, fw, tb, te):
    n = xt.shape[1]
    kernel = functools.partial(_peer_kernel, te=te)
    rt = pl.BlockSpec((PEER_HEADS, N_KEYS, tb), lambda t, e: (0, 0, t))
    return pl.pallas_call(
        kernel,
        grid=(n // tb, N_EXPERTS // te),
        in_specs=[pl.BlockSpec((D_MODEL, tb), lambda t, e: (0, t)),
                  pl.BlockSpec((te, D_MODEL), lambda t, e: (e, 0)),
                  pl.BlockSpec((D_MODEL, te), lambda t, e: (0, e)),
                  rt, rt, rt, rt,
                  pl.BlockSpec((tb, D_MODEL), lambda t, e: (t, 0)),
                  pl.BlockSpec((1, D_MODEL), lambda t, e: (0, 0))],
        out_specs=pl.BlockSpec((tb, D_MODEL), lambda t, e: (t, 0)),
        out_shape=jax.ShapeDtypeStruct((n, D_MODEL), F32),
        scratch_shapes=[pltpu.VMEM((D_MODEL, tb), F32)],
        compiler_params=pltpu.CompilerParams(
            dimension_semantics=("arbitrary", "arbitrary"),
            vmem_limit_bytes=VMEM_LIMIT),
        name="peer_experts",
    )(xt, u_bf, vt_bf, lam, pw, r2, qw, h1, fw)


def _rope_tables(T):
    d = ATT_HEAD_DIM
    inv_freq = ROPE_THETA ** (-jnp.arange(0, d, 2, dtype=F32) / d)
    ang = jnp.arange(T, dtype=F32)[:, None] * inv_freq[None, :]
    ang = jnp.concatenate([ang, ang], axis=-1)
    sign = jnp.where(jnp.arange(d) < d // 2, -1.0, 1.0).astype(F32)
    cos = jnp.tile(jnp.cos(ang), (1, 2))
    sin = jnp.tile(jnp.sin(ang) * sign[None, :], (1, 2))
    return cos, sin


def _pick(n, prefs):
    for p in prefs:
        if n % p == 0:
            return p
    raise ValueError(f"no supported tile for extent {n}")


def kernel(x, meta_tokens, mix_norm_w, w_in, rec_lb_logits, rec_norm_w, diff_lambda_q1, diff_lambda_k1, diff_lambda_q2, diff_lambda_k2, diff_subln_w, w_out, ffn_norm_w, peer_w_query, peer_subkeys, peer_u, peer_v, final_norm_w):
    B, S, D = x.shape
    assert D == D_MODEL and w_in.shape[0] == 1 and S % REC_CHUNK == 0
    n = B * S
    x2 = x.reshape(n, D)

    cos, sin = _rope_tables(N_META + S)
    w_bf = w_in[0].astype(BF16)
    nw = mix_norm_w[0].reshape(1, D)

    tm = _pick(S, (512, 256, 128))
    q, k, v, rq, rf, ri, rg = _inproj(x2, nw, w_bf, cos[N_META:], sin[N_META:], tm)
    mq_a, mk_a, mv_a, mrq, mrf, mri, _ = _inproj(
        meta_tokens.astype(F32), nw, w_bf, cos[:N_META], sin[:N_META], N_META)

    bq = _pick(S, (256, 128))
    pad_m = lambda z: jnp.pad(z, ((0, HEAD_W - N_META), (0, 0)))
    lam4 = jnp.stack([diff_lambda_q1[0], diff_lambda_k1[0],
                      diff_lambda_q2[0], diff_lambda_k2[0]]).astype(F32)
    sw = diff_subln_w[0].reshape(1, HEAD_W)
    r3 = lambda z: z.reshape(B, S, GROUP_W)
    att = _attention(lam4, r3(q), r3(k), r3(v), pad_m(mk_a), pad_m(mv_a), sw, bq)

    lb = jax.nn.softmax(rec_lb_logits.astype(F32), axis=0)[0].reshape(1, GROUP_W)
    front = lambda z: jnp.pad(z, ((REC_CHUNK - N_META, 0), (0, 0)))
    rec = _hgrn(jnp.asarray(_cum_matrix()), lb, rec_norm_w[0].reshape(1, GROUP_W),
                r3(rq), r3(rf), r3(ri), r3(rg), front(mrq), front(mrf), front(mri))

    wqt = peer_w_query[0].T.astype(BF16)
    sk = peer_subkeys[0].reshape(2 * PEER_HEADS, N_KEYS, -1).astype(BF16)
    h1, xt, st = _outproj(x2, att.reshape(n, GROUP_W), rec.reshape(n, GROUP_W),
                          w_out[0].astype(BF16), ffn_norm_w[0].reshape(1, D), wqt, sk, tm)

    lam, pw, r2, qw = _route(st, _pick(n, (256, 128)))

    tb = _pick(n, (512, 256, 128))
    out = _peer(xt, peer_u[0].astype(BF16), peer_v[0].T.astype(BF16), lam, pw, r2, qw,
                h1, final_norm_w.reshape(1, D), tb, 512)
    return out.reshape(B, S, D)
```

```python
import functools
import itertools
import math

import numpy as np
import jax
import jax.numpy as jnp
from jax import lax
from jax.experimental import pallas as pl
from jax.experimental.pallas import tpu as pltpu

F32 = jnp.float32
BF16 = jnp.bfloat16

D_MODEL = 1024
N_META = 16
ATT_HEADS = 4
ATT_HEAD_DIM = 64
REC_HEADS = 4
HEAD_W = 128
GROUP_W = 512
N_GROUPS = 7
ROPE_THETA = 10000.0
PEER_HEADS = 8
N_KEYS = 128
N_EXPERTS = N_KEYS * N_KEYS
PEER_TOPK = 16
EPS = 1e-6
LAM_INIT = 0.8 - 0.6 * math.exp(-0.3 * 0)
NEG = -1e30
REC_CHUNK = 128
PEER_EXPERT_TILE = 1024
MXU_TILE = 256
VMEM_LIMIT = 56 * 1024 * 1024

_NT = (((1,), (1,)), ((), ()))


def _rms(x, w):
    return x * lax.rsqrt(jnp.mean(x * x, axis=-1, keepdims=True) + EPS) * w


def _inproj_kernel(x_ref, nw_ref, w_ref, cos_ref, sin_ref,
                   q_ref, k_ref, v_ref, rq_ref, rf_ref, ri_ref, rg_ref):
    a = _rms(x_ref[...], nw_ref[...]).astype(BF16)
    cos = cos_ref[...]
    sin = sin_ref[...]
    lane = lax.broadcasted_iota(jnp.int32, (1, HEAD_W), 1)
    lo = (lane % ATT_HEAD_DIM) < (ATT_HEAD_DIM // 2)

    def proj(c):
        return jnp.dot(a, w_ref[:, c * GROUP_W:(c + 1) * GROUP_W],
                       preferred_element_type=F32)

    def rope(y, scale):
        outs = []
        for h in range(ATT_HEADS):
            z = y[:, h * HEAD_W:(h + 1) * HEAD_W]
            rot = jnp.where(lo, pltpu.roll(z, HEAD_W - 32, 1), pltpu.roll(z, 32, 1))
            r = z * cos + rot * sin
            if scale is not None:
                r = r * scale
            outs.append(r)
        return jnp.concatenate(outs, axis=1)

    q_ref[...] = rope(proj(0), ATT_HEAD_DIM ** -0.5 * math.log2(math.e)).astype(q_ref.dtype)
    k_ref[...] = rope(proj(1), None).astype(k_ref.dtype)
    v_ref[...] = proj(2).astype(v_ref.dtype)
    rq_ref[...] = proj(3).astype(rq_ref.dtype)
    rf_ref[...] = proj(4).astype(rf_ref.dtype)
    ri_ref[...] = proj(5).astype(ri_ref.dtype)
    rg_ref[...] = proj(6).astype(rg_ref.dtype)


def _inproj(x2, nw, w_bf, cos_t, sin_t, tm):
    n = x2.shape[0]
    n_pos = cos_t.shape[0] // tm
    row = lambda i: (i, 0)
    fixed = lambda i: (0, 0)
    pos = lambda i: (i % n_pos, 0)
    dts = (BF16, BF16, BF16, BF16, F32, BF16, BF16)
    return pl.pallas_call(
        _inproj_kernel,
        grid=(n // tm,),
        in_specs=[pl.BlockSpec((tm, D_MODEL), row),
                  pl.BlockSpec((1, D_MODEL), fixed),
                  pl.BlockSpec((D_MODEL, N_GROUPS * GROUP_W), fixed),
                  pl.BlockSpec((tm, HEAD_W), pos),
                  pl.BlockSpec((tm, HEAD_W), pos)],
        out_specs=[pl.BlockSpec((tm, GROUP_W), row)] * N_GROUPS,
        out_shape=[jax.ShapeDtypeStruct((n, GROUP_W), dt) for dt in dts],
        compiler_params=pltpu.CompilerParams(
            dimension_semantics=("arbitrary",), vmem_limit_bytes=VMEM_LIMIT),
        name="inproj",
    )(x2, nw, w_bf, cos_t, sin_t)


def _attn_kernel(lam_ref, q_ref, k_ref, v_ref, km_ref, vm_ref, sw_ref, o_ref,
                 m_scr, acc_scr, s0_scr, s1_scr, *, bq):
    qi = pl.program_id(2)
    q = q_ref[...]
    lane = lax.broadcasted_iota(jnp.int32, (1, HEAD_W), 1)
    zero = jnp.zeros_like(q)
    qq = jnp.concatenate([jnp.where(lane < ATT_HEAD_DIM, q, zero),
                          jnp.where(lane >= ATT_HEAD_DIM, q, zero)], axis=0)

    m_scr[...] = jnp.full(m_scr.shape, NEG, F32)
    acc_scr[...] = jnp.zeros(acc_scr.shape, F32)

    def kv(ref, j):
        return ref[pl.ds(pl.multiple_of(j * bq, bq), bq), :]

    def scores(kb):
        return lax.dot_general(qq, kb, _NT, preferred_element_type=F32)

    def accumulate(s, vb, mask):
        if mask is not None:
            s = jnp.where(mask, s, NEG)
        m_prev = m_scr[...]
        m_new = jnp.maximum(m_prev, jnp.max(s, axis=1, keepdims=True))
        alpha = jnp.exp2(m_prev - m_new)
        p = jnp.exp2(s - jnp.concatenate([m_new] * (s.shape[1] // HEAD_W), axis=1))
        v_ext = jnp.concatenate([vb, jnp.ones_like(vb)], axis=1)
        acc_scr[...] = (jnp.concatenate([alpha, alpha], axis=1) * acc_scr[...]
                        + jnp.dot(p.astype(BF16), v_ext, preferred_element_type=F32))
        m_scr[...] = m_new

    col_m = lax.broadcasted_iota(jnp.int32, (1, km_ref.shape[0]), 1)
    accumulate(scores(km_ref[...]), vm_ref[...], col_m < N_META)

    s0_scr[...] = scores(kv(k_ref, 0))

    def body(jj, carry):
        j = 2 * jj
        s1_scr[...] = scores(kv(k_ref, j + 1))
        accumulate(s0_scr[...], kv(v_ref, j), None)
        s0_scr[...] = scores(kv(k_ref, j + 2))
        accumulate(s1_scr[...], kv(v_ref, j + 1), None)
        return carry

    lax.fori_loop(0, qi // 2, body, 0)

    row = lax.broadcasted_iota(jnp.int32, (2 * bq, bq), 0) % bq
    col = lax.broadcasted_iota(jnp.int32, (2 * bq, bq), 1)
    causal = col <= row

    @pl.when(qi % 2 == 0)
    def _():
        accumulate(s0_scr[...], kv(v_ref, qi), causal)

    @pl.when(qi % 2 == 1)
    def _():
        s1_scr[...] = scores(kv(k_ref, qi))
        accumulate(s0_scr[...], kv(v_ref, qi - 1), None)
        accumulate(s1_scr[...], kv(v_ref, qi), causal)

    lv = lam_ref[...]
    lam = (jnp.exp(jnp.sum(lv[0:1] * lv[1:2], axis=1, keepdims=True))
           - jnp.exp(jnp.sum(lv[2:3] * lv[3:4], axis=1, keepdims=True)) + LAM_INIT)
    acc = acc_scr[...]
    num = acc[:, :HEAD_W]
    den = acc[:, HEAD_W:]
    o = num[:bq] / den[:bq] - lam * (num[bq:] / den[bq:])
    o_ref[...] = (_rms(o, sw_ref[...]) * (1.0 - LAM_INIT)).astype(o_ref.dtype)


def _attention(lam4, q, k, v, km, vm, sw, bq):
    B, S, _ = q.shape
    kernel = functools.partial(_attn_kernel, bq=bq)
    return pl.pallas_call(
        kernel,
        grid=(B, ATT_HEADS, S // bq),
        in_specs=[pl.BlockSpec((4, ATT_HEAD_DIM), lambda b, h, i: (0, 0)),
                  pl.BlockSpec((None, bq, HEAD_W), lambda b, h, i: (b, i, h)),
                  pl.BlockSpec((None, S, HEAD_W), lambda b, h, i: (b, 0, h)),
                  pl.BlockSpec((None, S, HEAD_W), lambda b, h, i: (b, 0, h)),
                  pl.BlockSpec((km.shape[0], HEAD_W), lambda b, h, i: (0, h)),
                  pl.BlockSpec((km.shape[0], HEAD_W), lambda b, h, i: (0, h)),
                  pl.BlockSpec((1, HEAD_W), lambda b, h, i: (0, 0))],
        out_specs=pl.BlockSpec((None, bq, HEAD_W), lambda b, h, i: (b, i, h)),
        out_shape=jax.ShapeDtypeStruct((B, S, GROUP_W), BF16),
        scratch_shapes=[pltpu.VMEM((2 * bq, HEAD_W), F32),
                        pltpu.VMEM((2 * bq, 2 * HEAD_W), F32),
                        pltpu.VMEM((2 * bq, bq), F32),
                        pltpu.VMEM((2 * bq, bq), F32)],
        compiler_params=pltpu.CompilerParams(
            dimension_semantics=("arbitrary", "arbitrary", "arbitrary"),
            vmem_limit_bytes=VMEM_LIMIT),
        name="diff_attention",
    )(lam4, q, k, v, km, vm, sw)


def _cum_matrix():
    c = REC_CHUNK
    r = np.arange(c)
    tri = (r[None, :] <= r[:, None]).astype(np.float32)
    mats = [tri, tri[16 * (r // 16) + 7]]
    for half in (16, 32, 64):
        mats.append(tri[(r // (2 * half)) * (2 * half) + half - 1])
    return np.concatenate(mats, axis=0)


def _gates(rq, rf, lb):
    q = rq * jax.nn.sigmoid(rq)
    f = lb + (1.0 - lb) * jax.nn.sigmoid(rf)
    return q, 1.0 - f, jnp.log(f)


def _hgrn_kernel(cm_ref, lb_ref, nw_ref, rq_ref, rf_ref, ri_ref, rg_ref,
                 mq_ref, mf_ref, mi_ref, o_ref, st_scr):
    c_idx = pl.program_id(1)
    C = REC_CHUNK
    cm = cm_ref[...]

    def cumsums(g):
        return jnp.dot(cm, g, preferred_element_type=F32,
                       precision=lax.Precision.HIGHEST)

    def state_update(st, k, v32, cum):
        last = cum[C - 1:C]
        ke = (k * jnp.exp(last - cum)).astype(BF16)
        upd = jnp.dot(v32.T.astype(BF16), ke, preferred_element_type=F32)
        return st * jnp.exp(last) + upd

    @pl.when(c_idx == 0)
    def _():
        valid = lax.broadcasted_iota(jnp.int32, (C, 1), 0) >= C - N_META
        for h in range(REC_HEADS):
            sl = slice(h * HEAD_W, (h + 1) * HEAD_W)
            _, k, g = _gates(mq_ref[:, sl].astype(F32), mf_ref[:, sl], lb_ref[:, sl])
            k = jnp.where(valid, k, 0.0)
            g = jnp.where(valid, g, 0.0)
            v32 = jnp.where(valid, mi_ref[:, sl].astype(F32), 0.0)
            cum = cumsums(g)[:C]
            st_scr[h] = state_update(jnp.zeros((HEAD_W, HEAD_W), F32), k, v32, cum)

    t = lax.broadcasted_iota(jnp.int32, (C, C), 0)
    s = lax.broadcasted_iota(jnp.int32, (C, C), 1)
    mask_d = (t // 16 == s // 16) & (s <= t)
    masks = [(t // (2 * hf) == s // (2 * hf)) & ((t // hf) % 2 == 1) & ((s // hf) % 2 == 0)
             for hf in (16, 32, 64)]

    for h in range(REC_HEADS):
        sl = slice(h * HEAD_W, (h + 1) * HEAD_W)
        q, k, g = _gates(rq_ref[:, sl].astype(F32), rf_ref[:, sl], lb_ref[:, sl])
        v32 = ri_ref[:, sl].astype(F32)
        call = cumsums(g)
        cum = call[:C]
        st = st_scr[h]
        qe = (q * jnp.exp(cum)).astype(BF16)
        o = lax.dot_general(qe, st.astype(BF16), _NT, preferred_element_type=F32)

        def pair_scores(ref, clamp):
            dq = cum - ref
            dk = ref - cum
            if clamp:
                dq = jnp.minimum(dq, 0.0)
                dk = jnp.minimum(dk, 0.0)
            return lax.dot_general((q * jnp.exp(dq)).astype(BF16),
                                   (k * jnp.exp(dk)).astype(BF16), _NT,
                                   preferred_element_type=F32)

        a = jnp.where(mask_d, pair_scores(call[C:2 * C], False), 0.0)
        for lvl, mk in enumerate(masks):
            a = jnp.where(mk, pair_scores(call[(2 + lvl) * C:(3 + lvl) * C], True), a)
        o = o + jnp.dot(a.astype(BF16), v32.astype(BF16), preferred_element_type=F32)
        st_scr[h] = state_update(st, k, v32, cum)

        rg = rg_ref[:, sl].astype(F32)
        o = _rms(o, nw_ref[:, sl]) * (rg * jax.nn.sigmoid(rg))
        o_ref[:, sl] = o.astype(o_ref.dtype)


def _hgrn(cm, lb, nw, rq, rf, ri, rg, mq, mf, mi):
    B, S, _ = rq.shape
    C = REC_CHUNK
    tok = pl.BlockSpec((None, C, GROUP_W), lambda b, c: (b, c, 0))
    fix = lambda shape: pl.BlockSpec(shape, lambda b, c: (0, 0))
    return pl.pallas_call(
        _hgrn_kernel,
        grid=(B, S // C),
        in_specs=[fix(cm.shape), fix((1, GROUP_W)), fix((1, GROUP_W)),
                  tok, tok, tok, tok,
                  fix((C, GROUP_W)), fix((C, GROUP_W)), fix((C, GROUP_W))],
        out_specs=tok,
        out_shape=jax.ShapeDtypeStruct((B, S, GROUP_W), BF16),
        scratch_shapes=[pltpu.VMEM((REC_HEADS, HEAD_W, HEAD_W), F32)],
        compiler_params=pltpu.CompilerParams(
            dimension_semantics=("arbitrary", "arbitrary"),
            vmem_limit_bytes=VMEM_LIMIT),
        name="hgrn2",
    )(cm, lb, nw, rq, rf, ri, rg, mq, mf, mi)


def _outproj_kernel(x_ref, att_ref, rec_ref, wo_ref, fw_ref, wq_ref, sk_ref,
                    h_ref, xt_ref, st_ref):
    h1 = (x_ref[...]
          + jnp.dot(att_ref[...], wo_ref[:GROUP_W, :], preferred_element_type=F32)
          + jnp.dot(rec_ref[...], wo_ref[GROUP_W:, :], preferred_element_type=F32))
    h_ref[...] = h1
    xt = _rms(h1, fw_ref[...]).T.astype(BF16)
    xt_ref[...] = xt
    qt = jnp.dot(wq_ref[...], xt, preferred_element_type=F32).astype(BF16)
    for g in range(2 * PEER_HEADS):
        sl = slice(g * N_KEYS, (g + 1) * N_KEYS)
        st_ref[sl, :] = jnp.dot(sk_ref[g], qt[sl, :], preferred_element_type=F32)


def _outproj(x2, att, rec, wo, fw, wqt, sk, tm):
    n = x2.shape[0]
    row = lambda i: (i, 0)
    colb = lambda i: (0, i)
    fixed = lambda i: (0, 0)
    return pl.pallas_call(
        _outproj_kernel,
        grid=(n // tm,),
        in_specs=[pl.BlockSpec((tm, D_MODEL), row),
                  pl.BlockSpec((tm, GROUP_W), row),
                  pl.BlockSpec((tm, GROUP_W), row),
                  pl.BlockSpec((D_MODEL, D_MODEL), fixed),
                  pl.BlockSpec((1, D_MODEL), fixed),
                  pl.BlockSpec(wqt.shape, fixed),
                  pl.BlockSpec(sk.shape, lambda i: (0, 0, 0))],
        out_specs=[pl.BlockSpec((tm, D_MODEL), row),
                   pl.BlockSpec((D_MODEL, tm), colb),
                   pl.BlockSpec((wqt.shape[0], tm), colb)],
        out_shape=[jax.ShapeDtypeStruct((n, D_MODEL), F32),
                   jax.ShapeDtypeStruct((D_MODEL, n), BF16),
                   jax.ShapeDtypeStruct((wqt.shape[0], n), F32)],
        compiler_params=pltpu.CompilerParams(
            dimension_semantics=("arbitrary",), vmem_limit_bytes=VMEM_LIMIT),
        name="outproj_scores",
    )(x2, att, rec, wo, fw, wqt, sk)


def _staircase():
    return [(p, q) for p in range(PEER_TOPK) for q in range(PEER_TOPK)
            if (p + 1) * (q + 1) <= PEER_TOPK]


def _route_kernel(st_ref, lam_ref, pw_ref, r2_ref, qw_ref,
                  rank_scr, top_scr, cand_scr):
    K = PEER_TOPK
    tb = st_ref.shape[1]
    pairs = _staircase()

    for g in range(2 * PEER_HEADS):
        h, part = divmod(g, 2)
        s = st_ref[g * N_KEYS:(g + 1) * N_KEYS, :]

        def extract(p, carry):
            cur, rank = carry
            m = jnp.max(cur, axis=0, keepdims=True)
            hit = cur == m
            top_scr[part, p, h:h + 1, :] = m
            return jnp.where(hit, NEG, cur), jnp.where(hit, p.astype(F32), rank)

        _, rank = lax.fori_loop(0, K, extract, (s, jnp.full(s.shape, float(K), F32)))
        rank_scr[g] = rank

    a = [top_scr[0, p] for p in range(K)]
    b = [top_scr[1, p] for p in range(K)]
    sums = [a[p] + b[q] for (p, q) in pairs]
    for idx, v in enumerate(sums):
        cand_scr[idx] = v

    def kth(i, thr):
        cur = cand_scr[...]
        m = jnp.max(cur, axis=0)
        cand_scr[...] = jnp.where(cur == m[None], NEG, cur)
        return m

    thr = lax.fori_loop(0, K, kth, jnp.zeros((PEER_HEADS, tb), F32))
    m0 = sums[0]
    z = jnp.zeros((PEER_HEADS, tb), F32)
    lam = [jnp.zeros((PEER_HEADS, tb), F32) for _ in range(K)]
    for (p, q), v in zip(pairs, sums):
        sel = v >= thr
        z = z + jnp.where(sel, jnp.exp(v - m0), 0.0)
        lam[p] = lam[p] + jnp.where(sel, 1.0, 0.0)
    inv_z = 1.0 / z

    for h in range(PEER_HEADS):
        r1 = rank_scr[2 * h]
        lam_k = jnp.zeros((N_KEYS, tb), F32)
        for p in range(K):
            lam_k = jnp.where(r1 == float(p), lam[p][h:h + 1, :], lam_k)
        lam_ref[h] = lam_k
        s1 = st_ref[(2 * h) * N_KEYS:(2 * h + 1) * N_KEYS, :]
        pw_ref[h] = jnp.exp(s1 - a[0][h:h + 1, :]) * inv_z[h:h + 1, :]
        s2 = st_ref[(2 * h + 1) * N_KEYS:(2 * h + 2) * N_KEYS, :]
        qw_ref[h] = jnp.exp(s2 - b[0][h:h + 1, :]).astype(qw_ref.dtype)
        r2_ref[h] = rank_scr[2 * h + 1].astype(r2_ref.dtype)


def _route(st, tb):
    n = st.shape[1]
    n_pairs = len(_staircase())
    spec = pl.BlockSpec((PEER_HEADS, N_KEYS, tb), lambda i: (0, 0, i))
    shp = lambda dt: jax.ShapeDtypeStruct((PEER_HEADS, N_KEYS, n), dt)
    return pl.pallas_call(
        _route_kernel,
        grid=(n // tb,),
        in_specs=[pl.BlockSpec((st.shape[0], tb), lambda i: (0, i))],
        out_specs=[spec, spec, spec, spec],
        out_shape=[shp(F32), shp(F32), shp(BF16), shp(BF16)],
        scratch_shapes=[pltpu.VMEM((2 * PEER_HEADS, N_KEYS, tb), F32),
                        pltpu.VMEM((2, PEER_TOPK, PEER_HEADS, tb), F32),
                        pltpu.VMEM((n_pairs, PEER_HEADS, tb), F32)],
        compiler_params=pltpu.CompilerParams(
            dimension_semantics=("arbitrary",), vmem_limit_bytes=VMEM_LIMIT),
        name="peer_route",
    )(st)


def _peer_kernel(xt_ref, u_ref, vt_ref, lam_ref, pw_ref, r2_ref, qw_ref,
                 h_ref, fw_ref, o_ref, acc_scr, hid0_scr, hid1_scr, *, te, n_e):
    e = pl.program_id(1)
    tb = xt_ref.shape[1]
    n_grp = te // N_KEYS
    hid_bufs = (hid0_scr, hid1_scr)

    def hidden_pieces(par):
        hid_scr = hid_bufs[par]
        pieces = []
        for ii in range(n_grp):
            def piece(ii=ii):
                rows = slice(ii * N_KEYS, (ii + 1) * N_KEYS)
                i = e * n_grp + ii
                ht = jnp.dot(u_ref[rows, :], xt_ref[...], preferred_element_type=F32)
                act = (0.5 * ht * (1.0 + lax.erf(ht * (2.0 ** -0.5)))).astype(BF16)
                gate = jnp.zeros((N_KEYS, tb), BF16)
                for h in range(PEER_HEADS):
                    lam_b = jnp.broadcast_to(lam_ref[h, pl.ds(i, 1), :].astype(BF16), (N_KEYS, tb))
                    p_b = jnp.broadcast_to(pw_ref[h, pl.ds(i, 1), :].astype(BF16), (N_KEYS, tb))
                    gate = gate + jnp.where(r2_ref[h] < lam_b, qw_ref[h], jnp.zeros_like(p_b)) * p_b
                hid_scr[rows, :] = act * gate
            pieces.append(piece)
        return pieces

    def value_pieces(par):
        hid_scr = hid_bufs[par]
        k_tiles = te // MXU_TILE
        pieces = []
        for nt in range(tb // MXU_TILE):
            cols = slice(nt * MXU_TILE, (nt + 1) * MXU_TILE)
            state = {}
            for kt in range(k_tiles):
                ks = slice(kt * MXU_TILE, (kt + 1) * MXU_TILE)

                def piece(ks=ks, cols=cols, kt=kt, state=state):
                    part = jnp.dot(vt_ref[:, ks], hid_scr[ks, cols], preferred_element_type=F32)
                    state["sum"] = part if kt == 0 else state["sum"] + part
                    if kt == k_tiles - 1:
                        acc_scr[:, cols] += state.pop("sum")
                pieces.append(piece)
        return pieces

    def run(*stages):
        order = sorted(((k + 0.5) / len(st), s_idx, k)
                       for s_idx, st in enumerate(stages) for k in range(len(st)))
        for _, s_idx, k in order:
            stages[s_idx][k]()

    @pl.when(e == 0)
    def _():
        acc_scr[...] = jnp.zeros(acc_scr.shape, F32)
        run(hidden_pieces(0))

    for par in (0, 1):
        @pl.when((e >= 1) & (e < n_e) & (e % 2 == par))
        def _(par=par):
            run(value_pieces(1 - par), hidden_pieces(par))

    @pl.when(e == n_e)
    def _():
        run(value_pieces(1 - n_e % 2))
        o_ref[...] = _rms(h_ref[...] + acc_scr[...].T, fw_ref[...])


def _peer(xt, u_bf, vt_bf, lam, pw, r2, qw, h1, fw, tb, te):
    n = xt.shape[1]
    n_e = N_EXPERTS // te
    assert n_e >= 2
    kernel = functools.partial(_peer_kernel, te=te, n_e=n_e)
    rt = pl.BlockSpec((PEER_HEADS, N_KEYS, tb), lambda t, e: (0, 0, t))
    return pl.pallas_call(
        kernel,
        grid=(n // tb, n_e + 1),
        in_specs=[pl.BlockSpec((D_MODEL, tb), lambda t, e: (0, t)),
                  pl.BlockSpec((te, D_MODEL), lambda t, e: (jnp.minimum(e, n_e - 1), 0)),
                  pl.BlockSpec((D_MODEL, te), lambda t, e: (0, jnp.maximum(e - 1, 0))),
                  rt, rt, rt, rt,
                  pl.BlockSpec((tb, D_MODEL), lambda t, e: (t, 0)),
                  pl.BlockSpec((1, D_MODEL), lambda t, e: (0, 0))],
        out_specs=pl.BlockSpec((tb, D_MODEL), lambda t, e: (t, 0)),
        out_shape=jax.ShapeDtypeStruct((n, D_MODEL), F32),
        scratch_shapes=[pltpu.VMEM((D_MODEL, tb), F32),
                        pltpu.VMEM((te, tb), BF16), pltpu.VMEM((te, tb), BF16)],
        compiler_params=pltpu.CompilerParams(
            dimension_semantics=("arbitrary", "arbitrary"),
            vmem_limit_bytes=VMEM_LIMIT),
        name="peer_experts",
    )(xt, u_bf, vt_bf, lam, pw, r2, qw, h1, fw)


def _rope_tables(T):
    d = ATT_HEAD_DIM
    inv_freq = ROPE_THETA ** (-jnp.arange(0, d, 2, dtype=F32) / d)
    ang = jnp.arange(T, dtype=F32)[:, None] * inv_freq[None, :]
    ang = jnp.concatenate([ang, ang], axis=-1)
    sign = jnp.where(jnp.arange(d) < d // 2, -1.0, 1.0).astype(F32)
    cos = jnp.tile(jnp.cos(ang), (1, 2))
    sin = jnp.tile(jnp.sin(ang) * sign[None, :], (1, 2))
    return cos, sin


def _pick(n, prefs):
    for p in prefs:
        if n % p == 0:
            return p
    raise ValueError(f"no supported tile for extent {n}")


def kernel(x, meta_tokens, mix_norm_w, w_in, rec_lb_logits, rec_norm_w, diff_lambda_q1, diff_lambda_k1, diff_lambda_q2, diff_lambda_k2, diff_subln_w, w_out, ffn_norm_w, peer_w_query, peer_subkeys, peer_u, peer_v, final_norm_w):
    B, S, D = x.shape
    assert D == D_MODEL and w_in.shape[0] == 1 and S % REC_CHUNK == 0
    n = B * S
    x2 = x.reshape(n, D)

    cos, sin = _rope_tables(N_META + S)
    w_bf = w_in[0].astype(BF16)
    nw = mix_norm_w[0].reshape(1, D)

    tm = _pick(S, (512, 256, 128))
    q, k, v, rq, rf, ri, rg = _inproj(x2, nw, w_bf, cos[N_META:], sin[N_META:], tm)
    mq_a, mk_a, mv_a, mrq, mrf, mri, _ = _inproj(
        meta_tokens.astype(F32), nw, w_bf, cos[:N_META], sin[:N_META], N_META)

    bq = _pick(S, (256, 128))
    pad_m = lambda z: jnp.pad(z, ((0, HEAD_W - N_META), (0, 0)))
    lam4 = jnp.stack([diff_lambda_q1[0], diff_lambda_k1[0],
                      diff_lambda_q2[0], diff_lambda_k2[0]]).astype(F32)
    sw = diff_subln_w[0].reshape(1, HEAD_W)
    r3 = lambda z: z.reshape(B, S, GROUP_W)
    att = _attention(lam4, r3(q), r3(k), r3(v), pad_m(mk_a), pad_m(mv_a), sw, bq)

    lb = jax.nn.softmax(rec_lb_logits.astype(F32), axis=0)[0].reshape(1, GROUP_W)
    front = lambda z: jnp.pad(z, ((REC_CHUNK - N_META, 0), (0, 0)))
    rec = _hgrn(jnp.asarray(_cum_matrix()), lb, rec_norm_w[0].reshape(1, GROUP_W),
                r3(rq), r3(rf), r3(ri), r3(rg), front(mrq), front(mrf), front(mri))

    wqt = peer_w_query[0].T.astype(BF16)
    sk = peer_subkeys[0].reshape(2 * PEER_HEADS, N_KEYS, -1).astype(BF16)
    h1, xt, st = _outproj(x2, att.reshape(n, GROUP_W), rec.reshape(n, GROUP_W),
                          w_out[0].astype(BF16), ffn_norm_w[0].reshape(1, D), wqt, sk, tm)

    lam, pw, r2, qw = _route(st, _pick(n, (128,)))

    tb = _pick(n, (512, 256, 128))
    out = _peer(xt, peer_u[0].astype(BF16), peer_v[0].T.astype(BF16), lam, pw, r2, qw,
                h1, final_norm_w.reshape(1, D), tb, PEER_EXPERT_TILE)
    return out.reshape(B, S, D)
```

```python
import functools
import itertools
import math

import numpy as np
import jax
import jax.numpy as jnp
from jax import lax
from jax.experimental import pallas as pl
from jax.experimental.pallas import tpu as pltpu

F32 = jnp.float32
BF16 = jnp.bfloat16

D_MODEL = 1024
N_META = 16
ATT_HEADS = 4
ATT_HEAD_DIM = 64
REC_HEADS = 4
HEAD_W = 128
GROUP_W = 512
N_GROUPS = 7
ROPE_THETA = 10000.0
PEER_HEADS = 8
N_KEYS = 128
N_EXPERTS = N_KEYS * N_KEYS
PEER_TOPK = 16
EPS = 1e-6
LAM_INIT = 0.8 - 0.6 * math.exp(-0.3 * 0)
NEG = -1e30
REC_CHUNK = 128
PEER_EXPERT_TILE = 1024
PEER_TOKEN_STRIP = 512
MXU_TILE = 256
VMEM_LIMIT = 56 * 1024 * 1024

_NT = (((1,), (1,)), ((), ()))


def _rms(x, w):
    return x * lax.rsqrt(jnp.mean(x * x, axis=-1, keepdims=True) + EPS) * w


def _inproj_kernel(x_ref, nw_ref, w_ref, cos_ref, sin_ref,
                   q_ref, k_ref, v_ref, rq_ref, rf_ref, ri_ref, rg_ref):
    a = _rms(x_ref[...], nw_ref[...]).astype(BF16)
    cos = cos_ref[...]
    sin = sin_ref[...]
    lane = lax.broadcasted_iota(jnp.int32, (1, HEAD_W), 1)
    lo = (lane % ATT_HEAD_DIM) < (ATT_HEAD_DIM // 2)

    def proj(c):
        return jnp.dot(a, w_ref[:, c * GROUP_W:(c + 1) * GROUP_W],
                       preferred_element_type=F32)

    def rope(y, scale):
        outs = []
        for h in range(ATT_HEADS):
            z = y[:, h * HEAD_W:(h + 1) * HEAD_W]
            rot = jnp.where(lo, pltpu.roll(z, HEAD_W - 32, 1), pltpu.roll(z, 32, 1))
            r = z * cos + rot * sin
            if scale is not None:
                r = r * scale
            outs.append(r)
        return jnp.concatenate(outs, axis=1)

    q_ref[...] = rope(proj(0), ATT_HEAD_DIM ** -0.5 * math.log2(math.e)).astype(q_ref.dtype)
    k_ref[...] = rope(proj(1), None).astype(k_ref.dtype)
    v_ref[...] = proj(2).astype(v_ref.dtype)
    rq_ref[...] = proj(3).astype(rq_ref.dtype)
    rf_ref[...] = proj(4).astype(rf_ref.dtype)
    ri_ref[...] = proj(5).astype(ri_ref.dtype)
    rg_ref[...] = proj(6).astype(rg_ref.dtype)


def _inproj(x2, nw, w_bf, cos_t, sin_t, tm):
    n = x2.shape[0]
    n_pos = cos_t.shape[0] // tm
    row = lambda i: (i, 0)
    fixed = lambda i: (0, 0)
    pos = lambda i: (i % n_pos, 0)
    dts = (BF16, BF16, BF16, BF16, F32, BF16, BF16)
    return pl.pallas_call(
        _inproj_kernel,
        grid=(n // tm,),
        in_specs=[pl.BlockSpec((tm, D_MODEL), row),
                  pl.BlockSpec((1, D_MODEL), fixed),
                  pl.BlockSpec((D_MODEL, N_GROUPS * GROUP_W), fixed),
                  pl.BlockSpec((tm, HEAD_W), pos),
                  pl.BlockSpec((tm, HEAD_W), pos)],
        out_specs=[pl.BlockSpec((tm, GROUP_W), row)] * N_GROUPS,
        out_shape=[jax.ShapeDtypeStruct((n, GROUP_W), dt) for dt in dts],
        compiler_params=pltpu.CompilerParams(
            dimension_semantics=("arbitrary",), vmem_limit_bytes=VMEM_LIMIT),
        name="inproj",
    )(x2, nw, w_bf, cos_t, sin_t)


def _attn_kernel(lam_ref, q_ref, k_ref, v_ref, km_ref, vm_ref, sw_ref, o_ref,
                 m_scr, acc_scr, s0_scr, s1_scr, *, bq):
    qi = pl.program_id(2)
    q = q_ref[...]
    lane = lax.broadcasted_iota(jnp.int32, (1, HEAD_W), 1)
    zero = jnp.zeros_like(q)
    qq = jnp.concatenate([jnp.where(lane < ATT_HEAD_DIM, q, zero),
                          jnp.where(lane >= ATT_HEAD_DIM, q, zero)], axis=0)

    m_scr[...] = jnp.full(m_scr.shape, NEG, F32)
    acc_scr[...] = jnp.zeros(acc_scr.shape, F32)

    def kv(ref, j):
        return ref[pl.ds(pl.multiple_of(j * bq, bq), bq), :]

    def scores(kb):
        return lax.dot_general(qq, kb, _NT, preferred_element_type=F32)

    def accumulate(s, vb, mask):
        if mask is not None:
            s = jnp.where(mask, s, NEG)
        m_prev = m_scr[...]
        m_new = jnp.maximum(m_prev, jnp.max(s, axis=1, keepdims=True))
        alpha = jnp.exp2(m_prev - m_new)
        p = jnp.exp2(s - jnp.concatenate([m_new] * (s.shape[1] // HEAD_W), axis=1))
        v_ext = jnp.concatenate([vb, jnp.ones_like(vb)], axis=1)
        acc_scr[...] = (jnp.concatenate([alpha, alpha], axis=1) * acc_scr[...]
                        + jnp.dot(p.astype(BF16), v_ext, preferred_element_type=F32))
        m_scr[...] = m_new

    col_m = lax.broadcasted_iota(jnp.int32, (1, km_ref.shape[0]), 1)
    accumulate(scores(km_ref[...]), vm_ref[...], col_m < N_META)

    s0_scr[...] = scores(kv(k_ref, 0))

    def body(jj, carry):
        j = 2 * jj
        s1_scr[...] = scores(kv(k_ref, j + 1))
        accumulate(s0_scr[...], kv(v_ref, j), None)
        s0_scr[...] = scores(kv(k_ref, j + 2))
        accumulate(s1_scr[...], kv(v_ref, j + 1), None)
        return carry

    lax.fori_loop(0, qi // 2, body, 0)

    row = lax.broadcasted_iota(jnp.int32, (2 * bq, bq), 0) % bq
    col = lax.broadcasted_iota(jnp.int32, (2 * bq, bq), 1)
    causal = col <= row

    @pl.when(qi % 2 == 0)
    def _():
        accumulate(s0_scr[...], kv(v_ref, qi), causal)

    @pl.when(qi % 2 == 1)
    def _():
        s1_scr[...] = scores(kv(k_ref, qi))
        accumulate(s0_scr[...], kv(v_ref, qi - 1), None)
        accumulate(s1_scr[...], kv(v_ref, qi), causal)

    lv = lam_ref[...]
    lam = (jnp.exp(jnp.sum(lv[0:1] * lv[1:2], axis=1, keepdims=True))
           - jnp.exp(jnp.sum(lv[2:3] * lv[3:4], axis=1, keepdims=True)) + LAM_INIT)
    acc = acc_scr[...]
    num = acc[:, :HEAD_W]
    den = acc[:, HEAD_W:]
    o = num[:bq] / den[:bq] - lam * (num[bq:] / den[bq:])
    o_ref[...] = (_rms(o, sw_ref[...]) * (1.0 - LAM_INIT)).astype(o_ref.dtype)


def _attention(lam4, q, k, v, km, vm, sw, bq):
    B, S, _ = q.shape
    kernel = functools.partial(_attn_kernel, bq=bq)
    return pl.pallas_call(
        kernel,
        grid=(B, ATT_HEADS, S // bq),
        in_specs=[pl.BlockSpec((4, ATT_HEAD_DIM), lambda b, h, i: (0, 0)),
                  pl.BlockSpec((None, bq, HEAD_W), lambda b, h, i: (b, i, h)),
                  pl.BlockSpec((None, S, HEAD_W), lambda b, h, i: (b, 0, h)),
                  pl.BlockSpec((None, S, HEAD_W), lambda b, h, i: (b, 0, h)),
                  pl.BlockSpec((km.shape[0], HEAD_W), lambda b, h, i: (0, h)),
                  pl.BlockSpec((km.shape[0], HEAD_W), lambda b, h, i: (0, h)),
                  pl.BlockSpec((1, HEAD_W), lambda b, h, i: (0, 0))],
        out_specs=pl.BlockSpec((None, bq, HEAD_W), lambda b, h, i: (b, i, h)),
        out_shape=jax.ShapeDtypeStruct((B, S, GROUP_W), BF16),
        scratch_shapes=[pltpu.VMEM((2 * bq, HEAD_W), F32),
                        pltpu.VMEM((2 * bq, 2 * HEAD_W), F32),
                        pltpu.VMEM((2 * bq, bq), F32),
                        pltpu.VMEM((2 * bq, bq), F32)],
        compiler_params=pltpu.CompilerParams(
            dimension_semantics=("arbitrary", "arbitrary", "arbitrary"),
            vmem_limit_bytes=VMEM_LIMIT),
        name="diff_attention",
    )(lam4, q, k, v, km, vm, sw)


def _cum_matrix():
    r = np.arange(REC_CHUNK)
    return (r[None, :] <= r[:, None]).astype(np.float32)


def _block_rows(cum, period, offset):
    return jnp.concatenate(
        [jnp.broadcast_to(cum[b * period + offset:b * period + offset + 1], (period, cum.shape[1]))
         for b in range(cum.shape[0] // period)], axis=0)


def _gates(rq, rf, lb):
    q = rq * jax.nn.sigmoid(rq)
    f = lb + (1.0 - lb) * jax.nn.sigmoid(rf)
    return q, 1.0 - f, jnp.log(f)


def _hgrn_kernel(cm_ref, lb_ref, nw_ref, rq_ref, rf_ref, ri_ref, rg_ref,
                 mq_ref, mf_ref, mi_ref, o_ref, st_scr):
    c_idx = pl.program_id(1)
    C = REC_CHUNK
    tri = cm_ref[...]

    def cumsum(g):
        g0 = g.astype(BF16)
        r1 = g - g0.astype(F32)
        g1 = r1.astype(BF16)
        g2 = (r1 - g1.astype(F32)).astype(BF16)
        return (jnp.dot(tri, g0, preferred_element_type=F32)
                + jnp.dot(tri, g1, preferred_element_type=F32)
                + jnp.dot(tri, g2, preferred_element_type=F32))

    def state_update(st, k, v32, cum):
        last = cum[C - 1:C]
        ke = (k * jnp.exp(last - cum)).astype(BF16)
        upd = jnp.dot(v32.T.astype(BF16), ke, preferred_element_type=F32)
        return st * jnp.exp(last) + upd

    @pl.when(c_idx == 0)
    def _():
        valid = lax.broadcasted_iota(jnp.int32, (C, 1), 0) >= C - N_META
        for h in range(REC_HEADS):
            sl = slice(h * HEAD_W, (h + 1) * HEAD_W)
            _, k, g = _gates(mq_ref[:, sl].astype(F32), mf_ref[:, sl], lb_ref[:, sl])
            k = jnp.where(valid, k, 0.0)
            g = jnp.where(valid, g, 0.0)
            v32 = jnp.where(valid, mi_ref[:, sl].astype(F32), 0.0)
            cum = cumsum(g)
            st_scr[h] = state_update(jnp.zeros((HEAD_W, HEAD_W), F32), k, v32, cum)

    t = lax.broadcasted_iota(jnp.int32, (C, C), 0)
    s = lax.broadcasted_iota(jnp.int32, (C, C), 1)
    mask_d = (t // 16 == s // 16) & (s <= t)
    masks = [(t // (2 * hf) == s // (2 * hf)) & ((t // hf) % 2 == 1) & ((s // hf) % 2 == 0)
             for hf in (16, 32, 64)]

    for h in range(REC_HEADS):
        sl = slice(h * HEAD_W, (h + 1) * HEAD_W)
        q, k, g = _gates(rq_ref[:, sl].astype(F32), rf_ref[:, sl], lb_ref[:, sl])
        v32 = ri_ref[:, sl].astype(F32)
        cum = cumsum(g)
        refs = [_block_rows(cum, 2 * hf, hf - 1) for hf in (16, 32, 64)]
        st = st_scr[h]
        qe = (q * jnp.exp(cum)).astype(BF16)
        o = lax.dot_general(qe, st.astype(BF16), _NT, preferred_element_type=F32)

        def pair_scores(ref, clamp):
            dq = cum - ref
            dk = ref - cum
            if clamp:
                dq = jnp.minimum(dq, 0.0)
                dk = jnp.minimum(dk, 0.0)
            return lax.dot_general((q * jnp.exp(dq)).astype(BF16),
                                   (k * jnp.exp(dk)).astype(BF16), _NT,
                                   preferred_element_type=F32)

        a = jnp.where(mask_d, pair_scores(_block_rows(cum, 16, 7), False), 0.0)
        for ref, mk in zip(refs, masks):
            a = jnp.where(mk, pair_scores(ref, True), a)
        o = o + jnp.dot(a.astype(BF16), v32.astype(BF16), preferred_element_type=F32)
        st_scr[h] = state_update(st, k, v32, cum)

        rg = rg_ref[:, sl].astype(F32)
        o = _rms(o, nw_ref[:, sl]) * (rg * jax.nn.sigmoid(rg))
        o_ref[:, sl] = o.astype(o_ref.dtype)


def _hgrn(cm, lb, nw, rq, rf, ri, rg, mq, mf, mi):
    B, S, _ = rq.shape
    C = REC_CHUNK
    tok = pl.BlockSpec((None, C, GROUP_W), lambda b, c: (b, c, 0))
    fix = lambda shape: pl.BlockSpec(shape, lambda b, c: (0, 0))
    return pl.pallas_call(
        _hgrn_kernel,
        grid=(B, S // C),
        in_specs=[fix(cm.shape), fix((1, GROUP_W)), fix((1, GROUP_W)),
                  tok, tok, tok, tok,
                  fix((C, GROUP_W)), fix((C, GROUP_W)), fix((C, GROUP_W))],
        out_specs=tok,
        out_shape=jax.ShapeDtypeStruct((B, S, GROUP_W), BF16),
        scratch_shapes=[pltpu.VMEM((REC_HEADS, HEAD_W, HEAD_W), F32)],
        compiler_params=pltpu.CompilerParams(
            dimension_semantics=("arbitrary", "arbitrary"),
            vmem_limit_bytes=VMEM_LIMIT),
        name="hgrn2",
    )(cm, lb, nw, rq, rf, ri, rg, mq, mf, mi)


def _outproj_kernel(x_ref, att_ref, rec_ref, wo_ref, fw_ref, wq_ref, sk_ref,
                    h_ref, xt_ref, st_ref):
    h1 = (x_ref[...]
          + jnp.dot(att_ref[...], wo_ref[:GROUP_W, :], preferred_element_type=F32)
          + jnp.dot(rec_ref[...], wo_ref[GROUP_W:, :], preferred_element_type=F32))
    h_ref[...] = h1
    xt = _rms(h1, fw_ref[...]).T.astype(BF16)
    xt_ref[...] = xt
    qt = jnp.dot(wq_ref[...], xt, preferred_element_type=F32).astype(BF16)
    for g in range(2 * PEER_HEADS):
        sl = slice(g * N_KEYS, (g + 1) * N_KEYS)
        st_ref[sl, :] = jnp.dot(sk_ref[g], qt[sl, :], preferred_element_type=F32)


def _outproj(x2, att, rec, wo, fw, wqt, sk, tm):
    n = x2.shape[0]
    row = lambda i: (i, 0)
    colb = lambda i: (0, i)
    fixed = lambda i: (0, 0)
    return pl.pallas_call(
        _outproj_kernel,
        grid=(n // tm,),
        in_specs=[pl.BlockSpec((tm, D_MODEL), row),
                  pl.BlockSpec((tm, GROUP_W), row),
                  pl.BlockSpec((tm, GROUP_W), row),
                  pl.BlockSpec((D_MODEL, D_MODEL), fixed),
                  pl.BlockSpec((1, D_MODEL), fixed),
                  pl.BlockSpec(wqt.shape, fixed),
                  pl.BlockSpec(sk.shape, lambda i: (0, 0, 0))],
        out_specs=[pl.BlockSpec((tm, D_MODEL), row),
                   pl.BlockSpec((D_MODEL, tm), colb),
                   pl.BlockSpec((wqt.shape[0], tm), colb)],
        out_shape=[jax.ShapeDtypeStruct((n, D_MODEL), F32),
                   jax.ShapeDtypeStruct((D_MODEL, n), BF16),
                   jax.ShapeDtypeStruct((wqt.shape[0], n), F32)],
        compiler_params=pltpu.CompilerParams(
            dimension_semantics=("arbitrary",), vmem_limit_bytes=VMEM_LIMIT),
        name="outproj_scores",
    )(x2, att, rec, wo, fw, wqt, sk)


def _staircase():
    return [(p, q) for p in range(PEER_TOPK) for q in range(PEER_TOPK)
            if (p + 1) * (q + 1) <= PEER_TOPK]


def _route_kernel(st_ref, lam_ref, pw_ref, r2_ref, qw_ref,
                  rank_scr, top_scr, cand_scr):
    K = PEER_TOPK
    tb = st_ref.shape[1]
    pairs = _staircase()

    for g in range(2 * PEER_HEADS):
        h, part = divmod(g, 2)
        s = st_ref[g * N_KEYS:(g + 1) * N_KEYS, :]

        def extract(p, carry, s=s, h=h, part=part):
            m_prev, rank = carry
            below = s < m_prev
            m = jnp.max(jnp.where(below, s, NEG), axis=0, keepdims=True)
            top_scr[part, p, h:h + 1, :] = m
            return m, rank + jnp.where(below, 1.0, 0.0)

        m_last, rank = lax.fori_loop(
            0, K, extract, (jnp.full((1, tb), -NEG, F32), jnp.full(s.shape, -1.0, F32)))
        rank_scr[g] = rank + jnp.where(s < m_last, 1.0, 0.0)

    a = [top_scr[0, p] for p in range(K)]
    b = [top_scr[1, p] for p in range(K)]
    sums = [a[p] + b[q] for (p, q) in pairs]
    for idx, v in enumerate(sums):
        cand_scr[idx] = v

    def kth(i, thr):
        cur = cand_scr[...]
        m = jnp.max(cur, axis=0)
        cand_scr[...] = jnp.where(cur == m[None], NEG, cur)
        return m

    thr = lax.fori_loop(0, K, kth, jnp.zeros((PEER_HEADS, tb), F32))
    m0 = sums[0]
    z = jnp.zeros((PEER_HEADS, tb), F32)
    lam = [jnp.zeros((PEER_HEADS, tb), F32) for _ in range(K)]
    for (p, q), v in zip(pairs, sums):
        sel = v >= thr
        z = z + jnp.where(sel, jnp.exp(v - m0), 0.0)
        lam[p] = lam[p] + jnp.where(sel, 1.0, 0.0)
    inv_z = 1.0 / z

    for h in range(PEER_HEADS):
        r1 = rank_scr[2 * h]
        lam_k = jnp.zeros((N_KEYS, tb), F32)
        for p in range(K):
            lam_k = jnp.where(r1 == float(p), lam[p][h:h + 1, :], lam_k)
        lam_ref[h] = lam_k
        s1 = st_ref[(2 * h) * N_KEYS:(2 * h + 1) * N_KEYS, :]
        pw_ref[h] = jnp.exp(s1 - a[0][h:h + 1, :]) * inv_z[h:h + 1, :]
        s2 = st_ref[(2 * h + 1) * N_KEYS:(2 * h + 2) * N_KEYS, :]
        qw_ref[h] = jnp.exp(s2 - b[0][h:h + 1, :]).astype(qw_ref.dtype)
        r2_ref[h] = rank_scr[2 * h + 1].astype(r2_ref.dtype)


def _route(st, tb):
    n = st.shape[1]
    n_pairs = len(_staircase())
    spec = pl.BlockSpec((PEER_HEADS, N_KEYS, tb), lambda i: (0, 0, i))
    shp = lambda dt: jax.ShapeDtypeStruct((PEER_HEADS, N_KEYS, n), dt)
    return pl.pallas_call(
        _route_kernel,
        grid=(n // tb,),
        in_specs=[pl.BlockSpec((st.shape[0], tb), lambda i: (0, i))],
        out_specs=[spec, spec, spec, spec],
        out_shape=[shp(F32), shp(F32), shp(BF16), shp(BF16)],
        scratch_shapes=[pltpu.VMEM((2 * PEER_HEADS, N_KEYS, tb), F32),
                        pltpu.VMEM((2, PEER_TOPK, PEER_HEADS, tb), F32),
                        pltpu.VMEM((n_pairs, PEER_HEADS, tb), F32)],
        compiler_params=pltpu.CompilerParams(
            dimension_semantics=("arbitrary",), vmem_limit_bytes=VMEM_LIMIT),
        name="peer_route",
    )(st)


def _peer_kernel(xt_ref, u_ref, vt_ref, lam_ref, pw_ref, r2_ref, qw_ref,
                 h_ref, fw_ref, o_ref, acc_scr, hid0_scr, hid1_scr, *, te, n_e):
    e = pl.program_id(1)
    tb = xt_ref.shape[1]
    n_grp = te // N_KEYS
    hid_bufs = (hid0_scr, hid1_scr)

    def hidden_pieces(par):
        hid_scr = hid_bufs[par]
        pieces = []
        tw = min(tb, PEER_TOKEN_STRIP)
        for ii, ct in itertools.product(range(n_grp), range(tb // tw)):
            def piece(ii=ii, ct=ct):
                rows = slice(ii * N_KEYS, (ii + 1) * N_KEYS)
                cols = slice(ct * tw, (ct + 1) * tw)
                i = e * n_grp + ii
                ht = jnp.dot(u_ref[rows, :], xt_ref[:, cols], preferred_element_type=F32)
                act = (0.5 * ht * (1.0 + lax.erf(ht * (2.0 ** -0.5)))).astype(BF16)
                gate = jnp.zeros((N_KEYS, tw), BF16)
                for h in range(PEER_HEADS):
                    lam_b = jnp.broadcast_to(
                        lam_ref[h, pl.ds(i, 1), :][:, cols].astype(BF16), (N_KEYS, tw))
                    p_b = jnp.broadcast_to(
                        pw_ref[h, pl.ds(i, 1), :][:, cols].astype(BF16), (N_KEYS, tw))
                    sel = jnp.where(r2_ref[h, :, cols] < lam_b, qw_ref[h, :, cols],
                                    jnp.zeros_like(p_b))
                    gate = gate + sel * p_b
                hid_scr[rows, cols] = act * gate
            pieces.append(piece)
        return pieces

    def value_pieces(par):
        hid_scr = hid_bufs[par]
        k_tiles = te // MXU_TILE
        pieces = []
        for nt in range(tb // MXU_TILE):
            cols = slice(nt * MXU_TILE, (nt + 1) * MXU_TILE)
            state = {}
            for kt in range(k_tiles):
                ks = slice(kt * MXU_TILE, (kt + 1) * MXU_TILE)

                def piece(ks=ks, cols=cols, kt=kt, state=state):
                    part = jnp.dot(vt_ref[:, ks], hid_scr[ks, cols], preferred_element_type=F32)
                    state["sum"] = part if kt == 0 else state["sum"] + part
                    if kt == k_tiles - 1:
                        acc_scr[:, cols] += state.pop("sum")
                pieces.append(piece)
        return pieces

    def run(*stages):
        order = sorted(((k + 0.5) / len(st), s_idx, k)
                       for s_idx, st in enumerate(stages) for k in range(len(st)))
        for _, s_idx, k in order:
            stages[s_idx][k]()

    @pl.when(e == 0)
    def _():
        acc_scr[...] = jnp.zeros(acc_scr.shape, F32)
        run(hidden_pieces(0))

    for par in (0, 1):
        @pl.when((e >= 1) & (e < n_e) & (e % 2 == par))
        def _(par=par):
            run(value_pieces(1 - par), hidden_pieces(par))

    @pl.when(e == n_e)
    def _():
        run(value_pieces(1 - n_e % 2))
        o_ref[...] = _rms(h_ref[...] + acc_scr[...].T, fw_ref[...])


def _peer(xt, u_bf, vt_bf, lam, pw, r2, qw, h1, fw, tb, te):
    n = xt.shape[1]
    n_e = N_EXPERTS // te
    assert n_e >= 2
    kernel = functools.partial(_peer_kernel, te=te, n_e=n_e)
    rt = pl.BlockSpec((PEER_HEADS, N_KEYS, tb), lambda t, e: (0, 0, t))
    return pl.pallas_call(
        kernel,
        grid=(n // tb, n_e + 1),
        in_specs=[pl.BlockSpec((D_MODEL, tb), lambda t, e: (0, t)),
                  pl.BlockSpec((te, D_MODEL), lambda t, e: (jnp.minimum(e, n_e - 1), 0)),
                  pl.BlockSpec((None, D_MODEL, te), lambda t, e: (jnp.maximum(e - 1, 0), 0, 0)),
                  rt, rt, rt, rt,
                  pl.BlockSpec((tb, D_MODEL), lambda t, e: (t, 0)),
                  pl.BlockSpec((1, D_MODEL), lambda t, e: (0, 0))],
        out_specs=pl.BlockSpec((tb, D_MODEL), lambda t, e: (t, 0)),
        out_shape=jax.ShapeDtypeStruct((n, D_MODEL), F32),
        scratch_shapes=[pltpu.VMEM((D_MODEL, tb), F32),
                        pltpu.VMEM((te, tb), BF16), pltpu.VMEM((te, tb), BF16)],
        compiler_params=pltpu.CompilerParams(
            dimension_semantics=("arbitrary", "arbitrary"),
            vmem_limit_bytes=VMEM_LIMIT),
        name="peer_experts",
    )(xt, u_bf, vt_bf, lam, pw, r2, qw, h1, fw)


def _rope_tables(T):
    d = ATT_HEAD_DIM
    inv_freq = ROPE_THETA ** (-jnp.arange(0, d, 2, dtype=F32) / d)
    ang = jnp.arange(T, dtype=F32)[:, None] * inv_freq[None, :]
    ang = jnp.concatenate([ang, ang], axis=-1)
    sign = jnp.where(jnp.arange(d) < d // 2, -1.0, 1.0).astype(F32)
    cos = jnp.tile(jnp.cos(ang), (1, 2))
    sin = jnp.tile(jnp.sin(ang) * sign[None, :], (1, 2))
    return cos, sin


def _pick(n, prefs):
    for p in prefs:
        if n % p == 0:
            return p
    raise ValueError(f"no supported tile for extent {n}")


def kernel(x, meta_tokens, mix_norm_w, w_in, rec_lb_logits, rec_norm_w, diff_lambda_q1, diff_lambda_k1, diff_lambda_q2, diff_lambda_k2, diff_subln_w, w_out, ffn_norm_w, peer_w_query, peer_subkeys, peer_u, peer_v, final_norm_w):
    B, S, D = x.shape
    assert D == D_MODEL and w_in.shape[0] == 1 and S % REC_CHUNK == 0
    n = B * S
    x2 = x.reshape(n, D)

    cos, sin = _rope_tables(N_META + S)
    w_bf = w_in[0].astype(BF16)
    nw = mix_norm_w[0].reshape(1, D)

    tm = _pick(S, (512, 256, 128))
    q, k, v, rq, rf, ri, rg = _inproj(x2, nw, w_bf, cos[N_META:], sin[N_META:], tm)
    mq_a, mk_a, mv_a, mrq, mrf, mri, _ = _inproj(
        meta_tokens.astype(F32), nw, w_bf, cos[:N_META], sin[:N_META], N_META)

    bq = _pick(S, (512, 256, 128))
    pad_m = lambda z: jnp.pad(z, ((0, HEAD_W - N_META), (0, 0)))
    lam4 = jnp.stack([diff_lambda_q1[0], diff_lambda_k1[0],
                      diff_lambda_q2[0], diff_lambda_k2[0]]).astype(F32)
    sw = diff_subln_w[0].reshape(1, HEAD_W)
    r3 = lambda z: z.reshape(B, S, GROUP_W)
    att = _attention(lam4, r3(q), r3(k), r3(v), pad_m(mk_a), pad_m(mv_a), sw, bq)

    lb = jax.nn.softmax(rec_lb_logits.astype(F32), axis=0)[0].reshape(1, GROUP_W)
    front = lambda z: jnp.pad(z, ((REC_CHUNK - N_META, 0), (0, 0)))
    rec = _hgrn(jnp.asarray(_cum_matrix(), BF16), lb, rec_norm_w[0].reshape(1, GROUP_W),
                r3(rq), r3(rf), r3(ri), r3(rg), front(mrq), front(mrf), front(mri))

    wqt = peer_w_query[0].T.astype(BF16)
    sk = peer_subkeys[0].reshape(2 * PEER_HEADS, N_KEYS, -1).astype(BF16)
    h1, xt, st = _outproj(x2, att.reshape(n, GROUP_W), rec.reshape(n, GROUP_W),
                          w_out[0].astype(BF16), ffn_norm_w[0].reshape(1, D), wqt, sk, tm)

    lam, pw, r2, qw = _route(st, _pick(n, (256, 128)))

    tb = _pick(n, (512, 256, 128))
    te = PEER_EXPERT_TILE
    vt_tiles = peer_v[0].astype(BF16).reshape(N_EXPERTS // te, te, D).transpose(0, 2, 1)
    out = _peer(xt, peer_u[0].astype(BF16), vt_tiles, lam, pw, r2, qw,
                h1, final_norm_w.reshape(1, D), tb, te)
    return out.reshape(B, S, D)
```

```python
import functools
import itertools
import math

import numpy as np
import jax
import jax.numpy as jnp
from jax import lax
from jax.experimental import pallas as pl
from jax.experimental.pallas import tpu as pltpu

F32 = jnp.float32
BF16 = jnp.bfloat16

D_MODEL = 1024
N_META = 16
ATT_HEADS = 4
ATT_HEAD_DIM = 64
REC_HEADS = 4
HEAD_W = 128
GROUP_W = 512
N_GROUPS = 7
ROPE_THETA = 10000.0
PEER_HEADS = 8
N_KEYS = 128
N_EXPERTS = N_KEYS * N_KEYS
PEER_TOPK = 16
EPS = 1e-6
LAM_INIT = 0.8 - 0.6 * math.exp(-0.3 * 0)
NEG = -1e30
REC_CHUNK = 128
PEER_EXPERT_TILE = 1024
PEER_TOKEN_STRIP = 512
PEER_HIDDEN_ROWS = 128
MXU_TILE = 256
VMEM_LIMIT = 56 * 1024 * 1024

_NT = (((1,), (1,)), ((), ()))


def _rms(x, w):
    return x * lax.rsqrt(jnp.mean(x * x, axis=-1, keepdims=True) + EPS) * w


def _inproj_kernel(x_ref, nw_ref, w_ref, cos_ref, sin_ref,
                   q_ref, k_ref, v_ref, rq_ref, rf_ref, ri_ref, rg_ref):
    a = _rms(x_ref[...], nw_ref[...]).astype(BF16)
    cos = cos_ref[...]
    sin = sin_ref[...]
    lane = lax.broadcasted_iota(jnp.int32, (1, HEAD_W), 1)
    lo = (lane % ATT_HEAD_DIM) < (ATT_HEAD_DIM // 2)

    def proj(c):
        return jnp.dot(a, w_ref[:, c * GROUP_W:(c + 1) * GROUP_W],
                       preferred_element_type=F32)

    def rope(y, scale):
        outs = []
        for h in range(ATT_HEADS):
            z = y[:, h * HEAD_W:(h + 1) * HEAD_W]
            rot = jnp.where(lo, pltpu.roll(z, HEAD_W - 32, 1), pltpu.roll(z, 32, 1))
            r = z * cos + rot * sin
            if scale is not None:
                r = r * scale
            outs.append(r)
        return jnp.concatenate(outs, axis=1)

    q_ref[...] = rope(proj(0), ATT_HEAD_DIM ** -0.5 * math.log2(math.e)).astype(q_ref.dtype)
    k_ref[...] = rope(proj(1), None).astype(k_ref.dtype)
    v_ref[...] = proj(2).astype(v_ref.dtype)
    rq_ref[...] = proj(3).astype(rq_ref.dtype)
    rf_ref[...] = proj(4).astype(rf_ref.dtype)
    ri_ref[...] = proj(5).astype(ri_ref.dtype)
    rg_ref[...] = proj(6).astype(rg_ref.dtype)


def _inproj(x2, nw, w_bf, cos_t, sin_t, tm):
    n = x2.shape[0]
    n_pos = cos_t.shape[0] // tm
    row = lambda i: (i, 0)
    fixed = lambda i: (0, 0)
    pos = lambda i: (i % n_pos, 0)
    dts = (BF16, BF16, BF16, BF16, F32, BF16, BF16)
    return pl.pallas_call(
        _inproj_kernel,
        grid=(n // tm,),
        in_specs=[pl.BlockSpec((tm, D_MODEL), row),
                  pl.BlockSpec((1, D_MODEL), fixed),
                  pl.BlockSpec((D_MODEL, N_GROUPS * GROUP_W), fixed),
                  pl.BlockSpec((tm, HEAD_W), pos),
                  pl.BlockSpec((tm, HEAD_W), pos)],
        out_specs=[pl.BlockSpec((tm, GROUP_W), row)] * N_GROUPS,
        out_shape=[jax.ShapeDtypeStruct((n, GROUP_W), dt) for dt in dts],
        compiler_params=pltpu.CompilerParams(
            dimension_semantics=("arbitrary",), vmem_limit_bytes=VMEM_LIMIT),
        name="inproj",
    )(x2, nw, w_bf, cos_t, sin_t)


def _attn_kernel(lam_ref, q_ref, k_ref, v_ref, km_ref, vm_ref, sw_ref, o_ref,
                 m_scr, acc_scr, s0_scr, s1_scr, *, bq):
    qi = pl.program_id(2)
    q = q_ref[...]
    lane = lax.broadcasted_iota(jnp.int32, (1, HEAD_W), 1)
    zero = jnp.zeros_like(q)
    qq = jnp.concatenate([jnp.where(lane < ATT_HEAD_DIM, q, zero),
                          jnp.where(lane >= ATT_HEAD_DIM, q, zero)], axis=0)

    m_scr[...] = jnp.full(m_scr.shape, NEG, F32)
    acc_scr[...] = jnp.zeros(acc_scr.shape, F32)

    def kv(ref, j):
        return ref[pl.ds(pl.multiple_of(j * bq, bq), bq), :]

    def scores(kb):
        return lax.dot_general(qq, kb, _NT, preferred_element_type=F32)

    def accumulate(s, vb, mask):
        if mask is not None:
            s = jnp.where(mask, s, NEG)
        m_prev = m_scr[...]
        m_new = jnp.maximum(m_prev, jnp.max(s, axis=1, keepdims=True))
        alpha = jnp.exp2(m_prev - m_new)
        p = jnp.exp2(s - jnp.concatenate([m_new] * (s.shape[1] // HEAD_W), axis=1))
        v_ext = jnp.concatenate([vb, jnp.ones_like(vb)], axis=1)
        acc_scr[...] = (jnp.concatenate([alpha, alpha], axis=1) * acc_scr[...]
                        + jnp.dot(p.astype(BF16), v_ext, preferred_element_type=F32))
        m_scr[...] = m_new

    col_m = lax.broadcasted_iota(jnp.int32, (1, km_ref.shape[0]), 1)
    accumulate(scores(km_ref[...]), vm_ref[...], col_m < N_META)

    s0_scr[...] = scores(kv(k_ref, 0))

    def body(jj, carry):
        j = 2 * jj
        s1_scr[...] = scores(kv(k_ref, j + 1))
        accumulate(s0_scr[...], kv(v_ref, j), None)
        s0_scr[...] = scores(kv(k_ref, j + 2))
        accumulate(s1_scr[...], kv(v_ref, j + 1), None)
        return carry

    lax.fori_loop(0, qi // 2, body, 0)

    row = lax.broadcasted_iota(jnp.int32, (2 * bq, bq), 0) % bq
    col = lax.broadcasted_iota(jnp.int32, (2 * bq, bq), 1)
    causal = col <= row

    @pl.when(qi % 2 == 0)
    def _():
        accumulate(s0_scr[...], kv(v_ref, qi), causal)

    @pl.when(qi % 2 == 1)
    def _():
        s1_scr[...] = scores(kv(k_ref, qi))
        accumulate(s0_scr[...], kv(v_ref, qi - 1), None)
        accumulate(s1_scr[...], kv(v_ref, qi), causal)

    lv = lam_ref[...]
    lam = (jnp.exp(jnp.sum(lv[0:1] * lv[1:2], axis=1, keepdims=True))
           - jnp.exp(jnp.sum(lv[2:3] * lv[3:4], axis=1, keepdims=True)) + LAM_INIT)
    acc = acc_scr[...]
    num = acc[:, :HEAD_W]
    den = acc[:, HEAD_W:]
    o = num[:bq] / den[:bq] - lam * (num[bq:] / den[bq:])
    o_ref[...] = (_rms(o, sw_ref[...]) * (1.0 - LAM_INIT)).astype(o_ref.dtype)


def _attention(lam4, q, k, v, km, vm, sw, bq):
    B, S, _ = q.shape
    kernel = functools.partial(_attn_kernel, bq=bq)
    return pl.pallas_call(
        kernel,
        grid=(B, ATT_HEADS, S // bq),
        in_specs=[pl.BlockSpec((4, ATT_HEAD_DIM), lambda b, h, i: (0, 0)),
                  pl.BlockSpec((None, bq, HEAD_W), lambda b, h, i: (b, i, h)),
                  pl.BlockSpec((None, S, HEAD_W), lambda b, h, i: (b, 0, h)),
                  pl.BlockSpec((None, S, HEAD_W), lambda b, h, i: (b, 0, h)),
                  pl.BlockSpec((km.shape[0], HEAD_W), lambda b, h, i: (0, h)),
                  pl.BlockSpec((km.shape[0], HEAD_W), lambda b, h, i: (0, h)),
                  pl.BlockSpec((1, HEAD_W), lambda b, h, i: (0, 0))],
        out_specs=pl.BlockSpec((None, bq, HEAD_W), lambda b, h, i: (b, i, h)),
        out_shape=jax.ShapeDtypeStruct((B, S, GROUP_W), BF16),
        scratch_shapes=[pltpu.VMEM((2 * bq, HEAD_W), F32),
                        pltpu.VMEM((2 * bq, 2 * HEAD_W), F32),
                        pltpu.VMEM((2 * bq, bq), F32),
                        pltpu.VMEM((2 * bq, bq), F32)],
        compiler_params=pltpu.CompilerParams(
            dimension_semantics=("arbitrary", "arbitrary", "arbitrary"),
            vmem_limit_bytes=VMEM_LIMIT),
        name="diff_attention",
    )(lam4, q, k, v, km, vm, sw)


def _cum_matrix():
    r = np.arange(REC_CHUNK)
    return (r[None, :] <= r[:, None]).astype(np.float32)


def _block_rows(cum, period, offset):
    return jnp.concatenate(
        [jnp.broadcast_to(cum[b * period + offset:b * period + offset + 1], (period, cum.shape[1]))
         for b in range(cum.shape[0] // period)], axis=0)


def _gates(rq, rf, lb):
    q = rq * jax.nn.sigmoid(rq)
    f = lb + (1.0 - lb) * jax.nn.sigmoid(rf)
    return q, 1.0 - f, jnp.log(f)


def _hgrn_kernel(cm_ref, lb_ref, nw_ref, rq_ref, rf_ref, ri_ref, rg_ref,
                 mq_ref, mf_ref, mi_ref, o_ref, st_scr):
    c_idx = pl.program_id(1)
    C = REC_CHUNK
    tri = cm_ref[...]

    def cumsum(g):
        g0 = g.astype(BF16)
        r1 = g - g0.astype(F32)
        g1 = r1.astype(BF16)
        g2 = (r1 - g1.astype(F32)).astype(BF16)
        return (jnp.dot(tri, g0, preferred_element_type=F32)
                + jnp.dot(tri, g1, preferred_element_type=F32)
                + jnp.dot(tri, g2, preferred_element_type=F32))

    def state_update(st, k, v32, cum):
        last = cum[C - 1:C]
        ke = (k * jnp.exp(last - cum)).astype(BF16)
        upd = jnp.dot(v32.T.astype(BF16), ke, preferred_element_type=F32)
        return st * jnp.exp(last) + upd

    @pl.when(c_idx == 0)
    def _():
        valid = lax.broadcasted_iota(jnp.int32, (C, 1), 0) >= C - N_META
        for h in range(REC_HEADS):
            sl = slice(h * HEAD_W, (h + 1) * HEAD_W)
            _, k, g = _gates(mq_ref[:, sl].astype(F32), mf_ref[:, sl], lb_ref[:, sl])
            k = jnp.where(valid, k, 0.0)
            g = jnp.where(valid, g, 0.0)
            v32 = jnp.where(valid, mi_ref[:, sl].astype(F32), 0.0)
            cum = cumsum(g)
            st_scr[h] = state_update(jnp.zeros((HEAD_W, HEAD_W), F32), k, v32, cum)

    t = lax.broadcasted_iota(jnp.int32, (C, C), 0)
    s = lax.broadcasted_iota(jnp.int32, (C, C), 1)
    mask_d = (t // 16 == s // 16) & (s <= t)
    masks = [(t // (2 * hf) == s // (2 * hf)) & ((t // hf) % 2 == 1) & ((s // hf) % 2 == 0)
             for hf in (16, 32, 64)]

    for h in range(REC_HEADS):
        sl = slice(h * HEAD_W, (h + 1) * HEAD_W)
        q, k, g = _gates(rq_ref[:, sl].astype(F32), rf_ref[:, sl], lb_ref[:, sl])
        v32 = ri_ref[:, sl].astype(F32)
        cum = cumsum(g)
        refs = [_block_rows(cum, 2 * hf, hf - 1) for hf in (16, 32, 64)]
        st = st_scr[h]
        qe = (q * jnp.exp(cum)).astype(BF16)
        o = lax.dot_general(qe, st.astype(BF16), _NT, preferred_element_type=F32)

        def pair_scores(ref, clamp):
            dq = cum - ref
            dk = ref - cum
            if clamp:
                dq = jnp.minimum(dq, 0.0)
                dk = jnp.minimum(dk, 0.0)
            return lax.dot_general((q * jnp.exp(dq)).astype(BF16),
                                   (k * jnp.exp(dk)).astype(BF16), _NT,
                                   preferred_element_type=F32)

        a = jnp.where(mask_d, pair_scores(_block_rows(cum, 16, 7), False), 0.0)
        for ref, mk in zip(refs, masks):
            a = jnp.where(mk, pair_scores(ref, True), a)
        o = o + jnp.dot(a.astype(BF16), v32.astype(BF16), preferred_element_type=F32)
        st_scr[h] = state_update(st, k, v32, cum)

        rg = rg_ref[:, sl].astype(F32)
        o = _rms(o, nw_ref[:, sl]) * (rg * jax.nn.sigmoid(rg))
        o_ref[:, sl] = o.astype(o_ref.dtype)


def _hgrn(cm, lb, nw, rq, rf, ri, rg, mq, mf, mi):
    B, S, _ = rq.shape
    C = REC_CHUNK
    tok = pl.BlockSpec((None, C, GROUP_W), lambda b, c: (b, c, 0))
    fix = lambda shape: pl.BlockSpec(shape, lambda b, c: (0, 0))
    return pl.pallas_call(
        _hgrn_kernel,
        grid=(B, S // C),
        in_specs=[fix(cm.shape), fix((1, GROUP_W)), fix((1, GROUP_W)),
                  tok, tok, tok, tok,
                  fix((C, GROUP_W)), fix((C, GROUP_W)), fix((C, GROUP_W))],
        out_specs=tok,
        out_shape=jax.ShapeDtypeStruct((B, S, GROUP_W), BF16),
        scratch_shapes=[pltpu.VMEM((REC_HEADS, HEAD_W, HEAD_W), F32)],
        compiler_params=pltpu.CompilerParams(
            dimension_semantics=("arbitrary", "arbitrary"),
            vmem_limit_bytes=VMEM_LIMIT),
        name="hgrn2",
    )(cm, lb, nw, rq, rf, ri, rg, mq, mf, mi)


def _outproj_kernel(x_ref, att_ref, rec_ref, wo_ref, fw_ref, wq_ref, sk_ref,
                    h_ref, xt_ref, st_ref):
    h1 = (x_ref[...]
          + jnp.dot(att_ref[...], wo_ref[:GROUP_W, :], preferred_element_type=F32)
          + jnp.dot(rec_ref[...], wo_ref[GROUP_W:, :], preferred_element_type=F32))
    h_ref[...] = h1
    xt = _rms(h1, fw_ref[...]).T.astype(BF16)
    xt_ref[...] = xt
    qt = jnp.dot(wq_ref[...], xt, preferred_element_type=F32).astype(BF16)
    for g in range(2 * PEER_HEADS):
        sl = slice(g * N_KEYS, (g + 1) * N_KEYS)
        st_ref[sl, :] = jnp.dot(sk_ref[g], qt[sl, :], preferred_element_type=F32)


def _outproj(x2, att, rec, wo, fw, wqt, sk, tm):
    n = x2.shape[0]
    row = lambda i: (i, 0)
    colb = lambda i: (0, i)
    fixed = lambda i: (0, 0)
    return pl.pallas_call(
        _outproj_kernel,
        grid=(n // tm,),
        in_specs=[pl.BlockSpec((tm, D_MODEL), row),
                  pl.BlockSpec((tm, GROUP_W), row),
                  pl.BlockSpec((tm, GROUP_W), row),
                  pl.BlockSpec((D_MODEL, D_MODEL), fixed),
                  pl.BlockSpec((1, D_MODEL), fixed),
                  pl.BlockSpec(wqt.shape, fixed),
                  pl.BlockSpec(sk.shape, lambda i: (0, 0, 0))],
        out_specs=[pl.BlockSpec((tm, D_MODEL), row),
                   pl.BlockSpec((D_MODEL, tm), colb),
                   pl.BlockSpec((wqt.shape[0], tm), colb)],
        out_shape=[jax.ShapeDtypeStruct((n, D_MODEL), F32),
                   jax.ShapeDtypeStruct((D_MODEL, n), BF16),
                   jax.ShapeDtypeStruct((wqt.shape[0], n), F32)],
        compiler_params=pltpu.CompilerParams(
            dimension_semantics=("arbitrary",), vmem_limit_bytes=VMEM_LIMIT),
        name="outproj_scores",
    )(x2, att, rec, wo, fw, wqt, sk)


def _staircase():
    return [(p, q) for p in range(PEER_TOPK) for q in range(PEER_TOPK)
            if (p + 1) * (q + 1) <= PEER_TOPK]


def _sort_network(n):
    pairs = []
    p = 1
    while p < n:
        k = p
        while k >= 1:
            for j in range(k % p, n - k, 2 * k):
                for i in range(min(k, n - j - k)):
                    if (i + j) // (2 * p) == (i + j + k) // (2 * p):
                        pairs.append((i + j, i + j + k))
            k //= 2
        p *= 2
    return pairs


def _sorted_desc(xs):
    xs = list(xs)
    for i, j in _sort_network(len(xs)):
        xs[i], xs[j] = jnp.maximum(xs[i], xs[j]), jnp.minimum(xs[i], xs[j])
    return xs


def _merge_top(a, b):
    n = len(a)
    xs = [jnp.maximum(a[i], b[n - 1 - i]) for i in range(n)]
    d = n // 2
    while d >= 1:
        for i in range(n):
            if (i // d) % 2 == 0:
                xs[i], xs[i + d] = jnp.maximum(xs[i], xs[i + d]), jnp.minimum(xs[i], xs[i + d])
        d //= 2
    return xs


def _route_kernel(st_ref, lam_ref, pw_ref, r2_ref, qw_ref, cand_scr):
    K = PEER_TOPK
    SUB = 8
    tb = st_ref.shape[1]
    pairs = _staircase()

    def key_slab(g, v):
        return st_ref[g * N_KEYS + v * SUB:g * N_KEYS + (v + 1) * SUB, :]

    tops = []
    for g in range(2 * PEER_HEADS):
        srt = _sorted_desc([key_slab(g, v) for v in range(N_KEYS // SUB)])
        for shift in (4, 2, 1):
            srt = _merge_top(srt, [pltpu.roll(x, shift, 0) for x in srt])
        tops.append(srt)

    sub = lax.broadcasted_iota(jnp.int32, (PEER_HEADS, tb), 0)

    def stack_heads(part, p):
        out = tops[part][p]
        for h in range(1, PEER_HEADS):
            out = jnp.where(sub == h, tops[2 * h + part][p], out)
        return out

    a = [stack_heads(0, p) for p in range(K)]
    b = [stack_heads(1, p) for p in range(K)]
    sums = [a[p] + b[q] for (p, q) in pairs]
    for idx, v in enumerate(sums):
        cand_scr[idx] = v

    def kth(i, thr):
        cur = cand_scr[...]
        m = jnp.max(cur, axis=0)
        cand_scr[...] = jnp.where(cur == m[None], NEG, cur)
        return m

    thr = lax.fori_loop(0, K, kth, jnp.zeros((PEER_HEADS, tb), F32))
    m0 = sums[0]
    z = jnp.zeros((PEER_HEADS, tb), F32)
    for v in sums:
        z = z + jnp.where(v >= thr, jnp.exp(v - m0), 0.0)
    inv_z = 1.0 / z

    for h in range(PEER_HEADS):
        top1, top2 = tops[2 * h], tops[2 * h + 1]
        thr_h = jnp.broadcast_to(thr[h:h + 1, :], (SUB, tb))
        inv_z_h = jnp.broadcast_to(inv_z[h:h + 1, :], (SUB, tb))
        for v in range(N_KEYS // SUB):
            rows = slice(v * SUB, (v + 1) * SUB)
            s1 = key_slab(2 * h, v)
            lam = jnp.zeros((SUB, tb), F32)
            for q in range(K):
                lam = lam + jnp.where(s1 + top2[q] >= thr_h, 1.0, 0.0)
            lam_ref[h, rows, :] = lam
            pw_ref[h, rows, :] = jnp.exp(s1 - top1[0]) * inv_z_h
        for v in range(0, N_KEYS // SUB, 2):
            rows = slice(v * SUB, (v + 2) * SUB)
            s2 = jnp.concatenate([key_slab(2 * h + 1, v), key_slab(2 * h + 1, v + 1)], axis=0)
            top2_w = [jnp.concatenate([t, t], axis=0) for t in top2]
            rank = jnp.zeros((2 * SUB, tb), F32)
            for p in range(K):
                rank = rank + jnp.where(s2 < top2_w[p], 1.0, 0.0)
            r2_ref[h, rows, :] = rank.astype(r2_ref.dtype)
            qw_ref[h, rows, :] = jnp.exp(s2 - top2_w[0]).astype(qw_ref.dtype)


def _route(st, tb):
    n = st.shape[1]
    n_pairs = len(_staircase())
    spec = pl.BlockSpec((PEER_HEADS, N_KEYS, tb), lambda i: (0, 0, i))
    shp = lambda dt: jax.ShapeDtypeStruct((PEER_HEADS, N_KEYS, n), dt)
    return pl.pallas_call(
        _route_kernel,
        grid=(n // tb,),
        in_specs=[pl.BlockSpec((st.shape[0], tb), lambda i: (0, i))],
        out_specs=[spec, spec, spec, spec],
        out_shape=[shp(F32), shp(F32), shp(BF16), shp(BF16)],
        scratch_shapes=[pltpu.VMEM((n_pairs, PEER_HEADS, tb), F32)],
        compiler_params=pltpu.CompilerParams(
            dimension_semantics=("arbitrary",), vmem_limit_bytes=VMEM_LIMIT),
        name="peer_route",
    )(st)


def _peer_kernel(xt_ref, u_ref, vt_ref, lam_ref, pw_ref, r2_ref, qw_ref,
                 h_ref, fw_ref, o_ref, acc_scr, hid0_scr, hid1_scr, *, te, n_e):
    e = pl.program_id(1)
    tb = xt_ref.shape[1]
    n_grp = te // N_KEYS
    hid_bufs = (hid0_scr, hid1_scr)

    def hidden_pieces(par):
        hid_scr = hid_bufs[par]
        pieces = []
        tw = min(tb, PEER_TOKEN_STRIP)
        gpp = PEER_HIDDEN_ROWS // N_KEYS
        for i0, ct in itertools.product(range(0, n_grp, gpp), range(tb // tw)):
            def piece(i0=i0, ct=ct):
                cols = slice(ct * tw, (ct + 1) * tw)
                prow = slice(i0 * N_KEYS, (i0 + gpp) * N_KEYS)
                ht_all = jnp.dot(u_ref[prow, :], xt_ref[:, cols], preferred_element_type=F32)
                for g in range(gpp):
                    rows = slice((i0 + g) * N_KEYS, (i0 + g + 1) * N_KEYS)
                    i = e * n_grp + i0 + g
                    ht = ht_all[g * N_KEYS:(g + 1) * N_KEYS]
                    act = (0.5 * ht * (1.0 + lax.erf(ht * (2.0 ** -0.5)))).astype(BF16)
                    gate = jnp.zeros((N_KEYS, tw), BF16)
                    for h in range(PEER_HEADS):
                        lam_b = jnp.broadcast_to(
                            lam_ref[h, pl.ds(i, 1), :][:, cols].astype(BF16), (N_KEYS, tw))
                        p_b = jnp.broadcast_to(
                            pw_ref[h, pl.ds(i, 1), :][:, cols].astype(BF16), (N_KEYS, tw))
                        sel = jnp.where(r2_ref[h, :, cols] < lam_b, qw_ref[h, :, cols],
                                        jnp.zeros_like(p_b))
                        gate = gate + sel * p_b
                    hid_scr[rows, cols] = act * gate
            pieces.append(piece)
        return pieces

    def value_pieces(par):
        hid_scr = hid_bufs[par]
        k_tiles = te // MXU_TILE
        pieces = []
        for nt in range(tb // MXU_TILE):
            cols = slice(nt * MXU_TILE, (nt + 1) * MXU_TILE)
            state = {}
            for kt in range(k_tiles):
                ks = slice(kt * MXU_TILE, (kt + 1) * MXU_TILE)

                def piece(ks=ks, cols=cols, kt=kt, state=state):
                    part = jnp.dot(vt_ref[:, ks], hid_scr[ks, cols], preferred_element_type=F32)
                    state["sum"] = part if kt == 0 else state["sum"] + part
                    if kt == k_tiles - 1:
                        acc_scr[:, cols] += state.pop("sum")
                pieces.append(piece)
        return pieces

    def run(*stages):
        order = sorted(((k + 0.5) / len(st), s_idx, k)
                       for s_idx, st in enumerate(stages) for k in range(len(st)))
        for _, s_idx, k in order:
            stages[s_idx][k]()

    @pl.when(e == 0)
    def _():
        acc_scr[...] = jnp.zeros(acc_scr.shape, F32)
        run(hidden_pieces(0))

    for par in (0, 1):
        @pl.when((e >= 1) & (e < n_e) & (e % 2 == par))
        def _(par=par):
            run(value_pieces(1 - par), hidden_pieces(par))

    @pl.when(e == n_e)
    def _():
        run(value_pieces(1 - n_e % 2))
        o_ref[...] = _rms(h_ref[...] + acc_scr[...].T, fw_ref[...])


def _peer(xt, u_bf, vt_bf, lam, pw, r2, qw, h1, fw, tb, te):
    n = xt.shape[1]
    n_e = N_EXPERTS // te
    assert n_e >= 2
    kernel = functools.partial(_peer_kernel, te=te, n_e=n_e)
    rt = pl.BlockSpec((PEER_HEADS, N_KEYS, tb), lambda t, e: (0, 0, t))
    return pl.pallas_call(
        kernel,
        grid=(n // tb, n_e + 1),
        in_specs=[pl.BlockSpec((D_MODEL, tb), lambda t, e: (0, t)),
                  pl.BlockSpec((te, D_MODEL), lambda t, e: (jnp.minimum(e, n_e - 1), 0)),
                  pl.BlockSpec((None, D_MODEL, te), lambda t, e: (jnp.maximum(e - 1, 0), 0, 0)),
                  rt, rt, rt, rt,
                  pl.BlockSpec((tb, D_MODEL), lambda t, e: (t, 0)),
                  pl.BlockSpec((1, D_MODEL), lambda t, e: (0, 0))],
        out_specs=pl.BlockSpec((tb, D_MODEL), lambda t, e: (t, 0)),
        out_shape=jax.ShapeDtypeStruct((n, D_MODEL), F32),
        scratch_shapes=[pltpu.VMEM((D_MODEL, tb), F32),
                        pltpu.VMEM((te, tb), BF16), pltpu.VMEM((te, tb), BF16)],
        compiler_params=pltpu.CompilerParams(
            dimension_semantics=("arbitrary", "arbitrary"),
            vmem_limit_bytes=VMEM_LIMIT),
        name="peer_experts",
    )(xt, u_bf, vt_bf, lam, pw, r2, qw, h1, fw)


def _rope_tables(T):
    d = ATT_HEAD_DIM
    inv_freq = ROPE_THETA ** (-jnp.arange(0, d, 2, dtype=F32) / d)
    ang = jnp.arange(T, dtype=F32)[:, None] * inv_freq[None, :]
    ang = jnp.concatenate([ang, ang], axis=-1)
    sign = jnp.where(jnp.arange(d) < d // 2, -1.0, 1.0).astype(F32)
    cos = jnp.tile(jnp.cos(ang), (1, 2))
    sin = jnp.tile(jnp.sin(ang) * sign[None, :], (1, 2))
    return cos, sin


def _pick(n, prefs):
    for p in prefs:
        if n % p == 0:
            return p
    raise ValueError(f"no supported tile for extent {n}")


def kernel(x, meta_tokens, mix_norm_w, w_in, rec_lb_logits, rec_norm_w, diff_lambda_q1, diff_lambda_k1, diff_lambda_q2, diff_lambda_k2, diff_subln_w, w_out, ffn_norm_w, peer_w_query, peer_subkeys, peer_u, peer_v, final_norm_w):
    B, S, D = x.shape
    assert D == D_MODEL and w_in.shape[0] == 1 and S % REC_CHUNK == 0
    n = B * S
    x2 = x.reshape(n, D)

    cos, sin = _rope_tables(N_META + S)
    w_bf = w_in[0].astype(BF16)
    nw = mix_norm_w[0].reshape(1, D)

    tm = _pick(S, (512, 256, 128))
    q, k, v, rq, rf, ri, rg = _inproj(x2, nw, w_bf, cos[N_META:], sin[N_META:], tm)
    mq_a, mk_a, mv_a, mrq, mrf, mri, _ = _inproj(
        meta_tokens.astype(F32), nw, w_bf, cos[:N_META], sin[:N_META], N_META)

    bq = _pick(S, (512, 256, 128))
    pad_m = lambda z: jnp.pad(z, ((0, HEAD_W - N_META), (0, 0)))
    lam4 = jnp.stack([diff_lambda_q1[0], diff_lambda_k1[0],
                      diff_lambda_q2[0], diff_lambda_k2[0]]).astype(F32)
    sw = diff_subln_w[0].reshape(1, HEAD_W)
    r3 = lambda z: z.reshape(B, S, GROUP_W)
    att = _attention(lam4, r3(q), r3(k), r3(v), pad_m(mk_a), pad_m(mv_a), sw, bq)

    lb = jax.nn.softmax(rec_lb_logits.astype(F32), axis=0)[0].reshape(1, GROUP_W)
    front = lambda z: jnp.pad(z, ((REC_CHUNK - N_META, 0), (0, 0)))
    rec = _hgrn(jnp.asarray(_cum_matrix(), BF16), lb, rec_norm_w[0].reshape(1, GROUP_W),
                r3(rq), r3(rf), r3(ri), r3(rg), front(mrq), front(mrf), front(mri))

    wqt = peer_w_query[0].T.astype(BF16)
    sk = peer_subkeys[0].reshape(2 * PEER_HEADS, N_KEYS, -1).astype(BF16)
    h1, xt, st = _outproj(x2, att.reshape(n, GROUP_W), rec.reshape(n, GROUP_W),
                          w_out[0].astype(BF16), ffn_norm_w[0].reshape(1, D), wqt, sk, tm)

    lam, pw, r2, qw = _route(st, _pick(n, (256, 128)))

    tb = _pick(n, (512, 256, 128))
    te = PEER_EXPERT_TILE
    vt_tiles = peer_v[0].astype(BF16).reshape(N_EXPERTS // te, te, D).transpose(0, 2, 1)
    out = _peer(xt, peer_u[0].astype(BF16), vt_tiles, lam, pw, r2, qw,
                h1, final_norm_w.reshape(1, D), tb, te)
    return out.reshape(B, S, D)
```

```python
import functools
import itertools
import math

import numpy as np
import jax
import jax.numpy as jnp
from jax import lax
from jax.experimental import pallas as pl
from jax.experimental.pallas import tpu as pltpu

F32 = jnp.float32
BF16 = jnp.bfloat16

D_MODEL = 1024
N_META = 16
ATT_HEADS = 4
ATT_HEAD_DIM = 64
REC_HEADS = 4
HEAD_W = 128
GROUP_W = 512
N_GROUPS = 7
ROPE_THETA = 10000.0
PEER_HEADS = 8
N_KEYS = 128
N_EXPERTS = N_KEYS * N_KEYS
PEER_TOPK = 16
EPS = 1e-6
LAM_INIT = 0.8 - 0.6 * math.exp(-0.3 * 0)
NEG = -1e30
REC_CHUNK = 128
PEER_EXPERT_TILE = 2048
PEER_TOKEN_STRIP = 512
PEER_HIDDEN_ROWS = 128
MXU_TILE = 256
VMEM_LIMIT = 56 * 1024 * 1024

_NT = (((1,), (1,)), ((), ()))


def _rms(x, w):
    return x * lax.rsqrt(jnp.mean(x * x, axis=-1, keepdims=True) + EPS) * w


def _inproj_kernel(x_ref, nw_ref, w_ref, cos_ref, sin_ref,
                   q_ref, k_ref, v_ref, rq_ref, rf_ref, ri_ref, rg_ref):
    a = _rms(x_ref[...], nw_ref[...]).astype(BF16)
    cos = cos_ref[...]
    sin = sin_ref[...]
    lane = lax.broadcasted_iota(jnp.int32, (1, HEAD_W), 1)
    lo = (lane % ATT_HEAD_DIM) < (ATT_HEAD_DIM // 2)

    def proj(c):
        return jnp.dot(a, w_ref[:, c * GROUP_W:(c + 1) * GROUP_W],
                       preferred_element_type=F32)

    def rope(y, scale):
        outs = []
        for h in range(ATT_HEADS):
            z = y[:, h * HEAD_W:(h + 1) * HEAD_W]
            rot = jnp.where(lo, pltpu.roll(z, HEAD_W - 32, 1), pltpu.roll(z, 32, 1))
            r = z * cos + rot * sin
            if scale is not None:
                r = r * scale
            outs.append(r)
        return jnp.concatenate(outs, axis=1)

    q_ref[...] = rope(proj(0), ATT_HEAD_DIM ** -0.5 * math.log2(math.e)).astype(q_ref.dtype)
    k_ref[...] = rope(proj(1), None).astype(k_ref.dtype)
    v_ref[...] = proj(2).astype(v_ref.dtype)
    rq_ref[...] = proj(3).astype(rq_ref.dtype)
    rf_ref[...] = proj(4).astype(rf_ref.dtype)
    ri_ref[...] = proj(5).astype(ri_ref.dtype)
    rg_ref[...] = proj(6).astype(rg_ref.dtype)


def _inproj(x2, nw, w_bf, cos_t, sin_t, tm):
    n = x2.shape[0]
    n_pos = cos_t.shape[0] // tm
    row = lambda i: (i, 0)
    fixed = lambda i: (0, 0)
    pos = lambda i: (i % n_pos, 0)
    dts = (BF16, BF16, BF16, BF16, F32, BF16, BF16)
    return pl.pallas_call(
        _inproj_kernel,
        grid=(n // tm,),
        in_specs=[pl.BlockSpec((tm, D_MODEL), row),
                  pl.BlockSpec((1, D_MODEL), fixed),
                  pl.BlockSpec((D_MODEL, N_GROUPS * GROUP_W), fixed),
                  pl.BlockSpec((tm, HEAD_W), pos),
                  pl.BlockSpec((tm, HEAD_W), pos)],
        out_specs=[pl.BlockSpec((tm, GROUP_W), row)] * N_GROUPS,
        out_shape=[jax.ShapeDtypeStruct((n, GROUP_W), dt) for dt in dts],
        compiler_params=pltpu.CompilerParams(
            dimension_semantics=("arbitrary",), vmem_limit_bytes=VMEM_LIMIT),
        name="inproj",
    )(x2, nw, w_bf, cos_t, sin_t)


def _attn_kernel(lam_ref, q_ref, k_ref, v_ref, km_ref, vm_ref, sw_ref, o_ref,
                 m_scr, acc_scr, s0_scr, s1_scr, *, bq):
    qi = pl.program_id(2)
    q = q_ref[...]
    lane = lax.broadcasted_iota(jnp.int32, (1, HEAD_W), 1)
    zero = jnp.zeros_like(q)
    qq = jnp.concatenate([jnp.where(lane < ATT_HEAD_DIM, q, zero),
                          jnp.where(lane >= ATT_HEAD_DIM, q, zero)], axis=0)

    m_scr[...] = jnp.full(m_scr.shape, NEG, F32)
    acc_scr[...] = jnp.zeros(acc_scr.shape, F32)

    def kv(ref, j):
        return ref[pl.ds(pl.multiple_of(j * bq, bq), bq), :]

    def scores(kb):
        return lax.dot_general(qq, kb, _NT, preferred_element_type=F32)

    def accumulate(s, vb, mask):
        if mask is not None:
            s = jnp.where(mask, s, NEG)
        m_prev = m_scr[...]
        m_new = jnp.maximum(m_prev, jnp.max(s, axis=1, keepdims=True))
        alpha = jnp.exp2(m_prev - m_new)
        p = jnp.exp2(s - jnp.concatenate([m_new] * (s.shape[1] // HEAD_W), axis=1))
        v_ext = jnp.concatenate([vb, jnp.ones_like(vb)], axis=1)
        acc_scr[...] = (jnp.concatenate([alpha, alpha], axis=1) * acc_scr[...]
                        + jnp.dot(p.astype(BF16), v_ext, preferred_element_type=F32))
        m_scr[...] = m_new

    col_m = lax.broadcasted_iota(jnp.int32, (1, km_ref.shape[0]), 1)
    accumulate(scores(km_ref[...]), vm_ref[...], col_m < N_META)

    s0_scr[...] = scores(kv(k_ref, 0))

    def body(jj, carry):
        j = 2 * jj
        s1_scr[...] = scores(kv(k_ref, j + 1))
        accumulate(s0_scr[...], kv(v_ref, j), None)
        s0_scr[...] = scores(kv(k_ref, j + 2))
        accumulate(s1_scr[...], kv(v_ref, j + 1), None)
        return carry

    lax.fori_loop(0, qi // 2, body, 0)

    row = lax.broadcasted_iota(jnp.int32, (2 * bq, bq), 0) % bq
    col = lax.broadcasted_iota(jnp.int32, (2 * bq, bq), 1)
    causal = col <= row

    @pl.when(qi % 2 == 0)
    def _():
        accumulate(s0_scr[...], kv(v_ref, qi), causal)

    @pl.when(qi % 2 == 1)
    def _():
        s1_scr[...] = scores(kv(k_ref, qi))
        accumulate(s0_scr[...], kv(v_ref, qi - 1), None)
        accumulate(s1_scr[...], kv(v_ref, qi), causal)

    lv = lam_ref[...]
    lam = (jnp.exp(jnp.sum(lv[0:1] * lv[1:2], axis=1, keepdims=True))
           - jnp.exp(jnp.sum(lv[2:3] * lv[3:4], axis=1, keepdims=True)) + LAM_INIT)
    acc = acc_scr[...]
    num = acc[:, :HEAD_W]
    den = acc[:, HEAD_W:]
    o = num[:bq] / den[:bq] - lam * (num[bq:] / den[bq:])
    o_ref[...] = (_rms(o, sw_ref[...]) * (1.0 - LAM_INIT)).astype(o_ref.dtype)


def _attention(lam4, q, k, v, km, vm, sw, bq):
    B, S, _ = q.shape
    kernel = functools.partial(_attn_kernel, bq=bq)
    return pl.pallas_call(
        kernel,
        grid=(B, ATT_HEADS, S // bq),
        in_specs=[pl.BlockSpec((4, ATT_HEAD_DIM), lambda b, h, i: (0, 0)),
                  pl.BlockSpec((None, bq, HEAD_W), lambda b, h, i: (b, i, h)),
                  pl.BlockSpec((None, S, HEAD_W), lambda b, h, i: (b, 0, h)),
                  pl.BlockSpec((None, S, HEAD_W), lambda b, h, i: (b, 0, h)),
                  pl.BlockSpec((km.shape[0], HEAD_W), lambda b, h, i: (0, h)),
                  pl.BlockSpec((km.shape[0], HEAD_W), lambda b, h, i: (0, h)),
                  pl.BlockSpec((1, HEAD_W), lambda b, h, i: (0, 0))],
        out_specs=pl.BlockSpec((None, bq, HEAD_W), lambda b, h, i: (b, i, h)),
        out_shape=jax.ShapeDtypeStruct((B, S, GROUP_W), BF16),
        scratch_shapes=[pltpu.VMEM((2 * bq, HEAD_W), F32),
                        pltpu.VMEM((2 * bq, 2 * HEAD_W), F32),
                        pltpu.VMEM((2 * bq, bq), F32),
                        pltpu.VMEM((2 * bq, bq), F32)],
        compiler_params=pltpu.CompilerParams(
            dimension_semantics=("arbitrary", "arbitrary", "arbitrary"),
            vmem_limit_bytes=VMEM_LIMIT),
        name="diff_attention",
    )(lam4, q, k, v, km, vm, sw)


def _cum_matrix():
    r = np.arange(REC_CHUNK)
    return (r[None, :] <= r[:, None]).astype(np.float32)


def _block_rows(cum, period, offset):
    return jnp.concatenate(
        [jnp.broadcast_to(cum[b * period + offset:b * period + offset + 1], (period, cum.shape[1]))
         for b in range(cum.shape[0] // period)], axis=0)


def _gates(rq, rf, lb):
    q = rq * jax.nn.sigmoid(rq)
    f = lb + (1.0 - lb) * jax.nn.sigmoid(rf)
    return q, 1.0 - f, jnp.log(f)


def _hgrn_kernel(cm_ref, lb_ref, nw_ref, rq_ref, rf_ref, ri_ref, rg_ref,
                 mq_ref, mf_ref, mi_ref, o_ref, st_scr):
    c_idx = pl.program_id(1)
    C = REC_CHUNK
    tri = cm_ref[...]

    def cumsum(g):
        g0 = g.astype(BF16)
        r1 = g - g0.astype(F32)
        g1 = r1.astype(BF16)
        g2 = (r1 - g1.astype(F32)).astype(BF16)
        return (jnp.dot(tri, g0, preferred_element_type=F32)
                + jnp.dot(tri, g1, preferred_element_type=F32)
                + jnp.dot(tri, g2, preferred_element_type=F32))

    def state_update(st, k, v32, cum):
        last = cum[C - 1:C]
        ke = (k * jnp.exp(last - cum)).astype(BF16)
        upd = jnp.dot(v32.T.astype(BF16), ke, preferred_element_type=F32)
        return st * jnp.exp(last) + upd

    @pl.when(c_idx == 0)
    def _():
        valid = lax.broadcasted_iota(jnp.int32, (C, 1), 0) >= C - N_META
        for h in range(REC_HEADS):
            sl = slice(h * HEAD_W, (h + 1) * HEAD_W)
            _, k, g = _gates(mq_ref[:, sl].astype(F32), mf_ref[:, sl], lb_ref[:, sl])
            k = jnp.where(valid, k, 0.0)
            g = jnp.where(valid, g, 0.0)
            v32 = jnp.where(valid, mi_ref[:, sl].astype(F32), 0.0)
            cum = cumsum(g)
            st_scr[h] = state_update(jnp.zeros((HEAD_W, HEAD_W), F32), k, v32, cum)

    t = lax.broadcasted_iota(jnp.int32, (C, C), 0)
    s = lax.broadcasted_iota(jnp.int32, (C, C), 1)
    mask_d = (t // 16 == s // 16) & (s <= t)
    masks = [(t // (2 * hf) == s // (2 * hf)) & ((t // hf) % 2 == 1) & ((s // hf) % 2 == 0)
             for hf in (16, 32, 64)]

    for h in range(REC_HEADS):
        sl = slice(h * HEAD_W, (h + 1) * HEAD_W)
        q, k, g = _gates(rq_ref[:, sl].astype(F32), rf_ref[:, sl], lb_ref[:, sl])
        v32 = ri_ref[:, sl].astype(F32)
        cum = cumsum(g)
        refs = [_block_rows(cum, 2 * hf, hf - 1) for hf in (16, 32, 64)]
        st = st_scr[h]
        qe = (q * jnp.exp(cum)).astype(BF16)
        o = lax.dot_general(qe, st.astype(BF16), _NT, preferred_element_type=F32)

        def pair_scores(ref, clamp):
            dq = cum - ref
            dk = ref - cum
            if clamp:
                dq = jnp.minimum(dq, 0.0)
                dk = jnp.minimum(dk, 0.0)
            return lax.dot_general((q * jnp.exp(dq)).astype(BF16),
                                   (k * jnp.exp(dk)).astype(BF16), _NT,
                                   preferred_element_type=F32)

        a = jnp.where(mask_d, pair_scores(_block_rows(cum, 16, 7), False), 0.0)
        for ref, mk in zip(refs, masks):
            a = jnp.where(mk, pair_scores(ref, True), a)
        o = o + jnp.dot(a.astype(BF16), v32.astype(BF16), preferred_element_type=F32)
        st_scr[h] = state_update(st, k, v32, cum)

        rg = rg_ref[:, sl].astype(F32)
        o = _rms(o, nw_ref[:, sl]) * (rg * jax.nn.sigmoid(rg))
        o_ref[:, sl] = o.astype(o_ref.dtype)


def _hgrn(cm, lb, nw, rq, rf, ri, rg, mq, mf, mi):
    B, S, _ = rq.shape
    C = REC_CHUNK
    tok = pl.BlockSpec((None, C, GROUP_W), lambda b, c: (b, c, 0))
    fix = lambda shape: pl.BlockSpec(shape, lambda b, c: (0, 0))
    return pl.pallas_call(
        _hgrn_kernel,
        grid=(B, S // C),
        in_specs=[fix(cm.shape), fix((1, GROUP_W)), fix((1, GROUP_W)),
                  tok, tok, tok, tok,
                  fix((C, GROUP_W)), fix((C, GROUP_W)), fix((C, GROUP_W))],
        out_specs=tok,
        out_shape=jax.ShapeDtypeStruct((B, S, GROUP_W), BF16),
        scratch_shapes=[pltpu.VMEM((REC_HEADS, HEAD_W, HEAD_W), F32)],
        compiler_params=pltpu.CompilerParams(
            dimension_semantics=("arbitrary", "arbitrary"),
            vmem_limit_bytes=VMEM_LIMIT),
        name="hgrn2",
    )(cm, lb, nw, rq, rf, ri, rg, mq, mf, mi)


def _outproj_kernel(x_ref, att_ref, rec_ref, wo_ref, fw_ref, wq_ref, sk_ref,
                    h_ref, xt_ref, st_ref):
    h1 = (x_ref[...]
          + jnp.dot(att_ref[...], wo_ref[:GROUP_W, :], preferred_element_type=F32)
          + jnp.dot(rec_ref[...], wo_ref[GROUP_W:, :], preferred_element_type=F32))
    h_ref[...] = h1
    xt = _rms(h1, fw_ref[...]).T.astype(BF16)
    xt_ref[...] = xt
    qt = jnp.dot(wq_ref[...], xt, preferred_element_type=F32).astype(BF16)
    for g in range(2 * PEER_HEADS):
        sl = slice(g * N_KEYS, (g + 1) * N_KEYS)
        st_ref[sl, :] = jnp.dot(sk_ref[g], qt[sl, :], preferred_element_type=F32)


def _outproj(x2, att, rec, wo, fw, wqt, sk, tm):
    n = x2.shape[0]
    row = lambda i: (i, 0)
    colb = lambda i: (0, i)
    fixed = lambda i: (0, 0)
    return pl.pallas_call(
        _outproj_kernel,
        grid=(n // tm,),
        in_specs=[pl.BlockSpec((tm, D_MODEL), row),
                  pl.BlockSpec((tm, GROUP_W), row),
                  pl.BlockSpec((tm, GROUP_W), row),
                  pl.BlockSpec((D_MODEL, D_MODEL), fixed),
                  pl.BlockSpec((1, D_MODEL), fixed),
                  pl.BlockSpec(wqt.shape, fixed),
                  pl.BlockSpec(sk.shape, lambda i: (0, 0, 0))],
        out_specs=[pl.BlockSpec((tm, D_MODEL), row),
                   pl.BlockSpec((D_MODEL, tm), colb),
                   pl.BlockSpec((wqt.shape[0], tm), colb)],
        out_shape=[jax.ShapeDtypeStruct((n, D_MODEL), F32),
                   jax.ShapeDtypeStruct((D_MODEL, n), BF16),
                   jax.ShapeDtypeStruct((wqt.shape[0], n), F32)],
        compiler_params=pltpu.CompilerParams(
            dimension_semantics=("arbitrary",), vmem_limit_bytes=VMEM_LIMIT),
        name="outproj_scores",
    )(x2, att, rec, wo, fw, wqt, sk)


def _staircase():
    return [(p, q) for p in range(PEER_TOPK) for q in range(PEER_TOPK)
            if (p + 1) * (q + 1) <= PEER_TOPK]


def _sort_network(n):
    pairs = []
    p = 1
    while p < n:
        k = p
        while k >= 1:
            for j in range(k % p, n - k, 2 * k):
                for i in range(min(k, n - j - k)):
                    if (i + j) // (2 * p) == (i + j + k) // (2 * p):
                        pairs.append((i + j, i + j + k))
            k //= 2
        p *= 2
    return pairs


def _sorted_desc(xs):
    xs = list(xs)
    for i, j in _sort_network(len(xs)):
        xs[i], xs[j] = jnp.maximum(xs[i], xs[j]), jnp.minimum(xs[i], xs[j])
    return xs


def _merge_top(a, b):
    n = len(a)
    xs = [jnp.maximum(a[i], b[n - 1 - i]) for i in range(n)]
    d = n // 2
    while d >= 1:
        for i in range(n):
            if (i // d) % 2 == 0:
                xs[i], xs[i + d] = jnp.maximum(xs[i], xs[i + d]), jnp.minimum(xs[i], xs[i + d])
        d //= 2
    return xs


def _prefix_count(tops, test):
    assert len(tops) == 16

    def pick(bits, lo):
        step = 8
        idx = [lo]
        for _ in bits:
            idx = [i + d for i in idx for d in (0, step)]
            step //= 2
        vals = [tops[i] for i in idx]
        for b in reversed(bits):
            vals = [jnp.where(b, vals[2 * k + 1], vals[2 * k]) for k in range(len(vals) // 2)]
        return vals[0]

    t1 = test(tops[7])
    t2 = test(pick([t1], 3))
    t3 = test(pick([t1, t2], 1))
    t4 = test(pick([t1, t2, t3], 0))
    t5 = test(tops[15])
    one = lambda t, w: jnp.where(t, float(w), 0.0)
    return one(t1, 8) + one(t2, 4) + one(t3, 2) + one(t4, 1) + one(t5, 1)


def _route_kernel(st_ref, lam_ref, pw_ref, r2_ref, qw_ref, thr_scr, top_scr):
    K = PEER_TOPK
    SUB = 8
    tb = st_ref.shape[1]
    pairs = _staircase()

    def key_slab(g, v):
        return st_ref[g * N_KEYS + v * SUB:g * N_KEYS + (v + 1) * SUB, :]

    for g in range(2 * PEER_HEADS):
        srt = _sorted_desc([key_slab(g, v) for v in range(N_KEYS // SUB)])
        for shift in (4, 2, 1):
            srt = _merge_top(srt, [pltpu.roll(x, shift, 0) for x in srt])
        for p in range(K):
            top_scr[g, p] = srt[p]

    class _Lazy:
        def __init__(self, g):
            self.g = g

        def __getitem__(self, p):
            return top_scr[self.g, p]

        def __len__(self):
            return K

    tops = [_Lazy(g) for g in range(2 * PEER_HEADS)]

    sub = lax.broadcasted_iota(jnp.int32, (PEER_HEADS, tb), 0)

    def stack_heads(part, p):
        out = tops[part][p]
        for h in range(1, PEER_HEADS):
            out = jnp.where(sub == h, tops[2 * h + part][p], out)
        return out

    a = [stack_heads(0, p) for p in range(K)]
    b = [stack_heads(1, p) for p in range(K)]
    sums = [a[p] + b[q] for (p, q) in pairs]
    n_full = len(sums) // K * K
    top = None
    for k in range(0, n_full, K):
        run = _sorted_desc(sums[k:k + K])
        top = run if top is None else _merge_top(top, run)
    for extra in sums[n_full:]:
        top = [jnp.maximum(top[0], extra)] + [
            jnp.maximum(top[i], jnp.minimum(top[i - 1], extra)) for i in range(1, K)]
    thr_scr[...] = top[K - 1]
    thr = thr_scr[...]
    m0 = sums[0]
    z = jnp.zeros((PEER_HEADS, tb), F32)
    for v in sums:
        z = z + jnp.where(v >= thr, jnp.exp(v - m0), 0.0)
    inv_z = 1.0 / z

    for h in range(PEER_HEADS):
        top1, top2 = tops[2 * h], tops[2 * h + 1]
        thr_h = jnp.broadcast_to(thr[h:h + 1, :], (SUB, tb))
        inv_z_h = jnp.broadcast_to(inv_z[h:h + 1, :], (SUB, tb))
        for v in range(N_KEYS // SUB):
            rows = slice(v * SUB, (v + 1) * SUB)
            s1 = key_slab(2 * h, v)
            lam_ref[h, rows, :] = _prefix_count(top2, lambda bq: s1 + bq >= thr_h)
            pw_ref[h, rows, :] = jnp.exp(s1 - top1[0]) * inv_z_h
        for v in range(0, N_KEYS // SUB, 2):
            rows = slice(v * SUB, (v + 2) * SUB)
            s2 = jnp.concatenate([key_slab(2 * h + 1, v), key_slab(2 * h + 1, v + 1)], axis=0)
            top2_w = [jnp.concatenate([top2[p], top2[p]], axis=0) for p in range(K)]
            rank = _prefix_count(top2_w, lambda bp: s2 < bp)
            r2_ref[h, rows, :] = rank.astype(r2_ref.dtype)
            qw_ref[h, rows, :] = jnp.exp(s2 - top2_w[0]).astype(qw_ref.dtype)


def _route(st, tb):
    n = st.shape[1]
    spec = pl.BlockSpec((PEER_HEADS, N_KEYS, tb), lambda i: (0, 0, i))
    shp = lambda dt: jax.ShapeDtypeStruct((PEER_HEADS, N_KEYS, n), dt)
    return pl.pallas_call(
        _route_kernel,
        grid=(n // tb,),
        in_specs=[pl.BlockSpec((st.shape[0], tb), lambda i: (0, i))],
        out_specs=[spec, spec, spec, spec],
        out_shape=[shp(F32), shp(F32), shp(BF16), shp(BF16)],
        scratch_shapes=[pltpu.VMEM((PEER_HEADS, tb), F32),
                        pltpu.VMEM((2 * PEER_HEADS, PEER_TOPK, 8, tb), F32)],
        compiler_params=pltpu.CompilerParams(
            dimension_semantics=("arbitrary",), vmem_limit_bytes=VMEM_LIMIT),
        name="peer_route",
    )(st)


def _peer_kernel(xt_ref, u_ref, vt_ref, lam_ref, pw_ref, r2_ref, qw_ref,
                 h_ref, fw_ref, o_ref, acc_scr, hid0_scr, hid1_scr, *, te, n_e):
    e = pl.program_id(1)
    tb = xt_ref.shape[1]
    n_grp = te // N_KEYS
    hid_bufs = (hid0_scr, hid1_scr)

    def hidden_pieces(par):
        hid_scr = hid_bufs[par]
        pieces = []
        tw = min(tb, PEER_TOKEN_STRIP)
        gpp = PEER_HIDDEN_ROWS // N_KEYS
        for i0, ct in itertools.product(range(0, n_grp, gpp), range(tb // tw)):
            def piece(i0=i0, ct=ct):
                cols = slice(ct * tw, (ct + 1) * tw)
                prow = slice(i0 * N_KEYS, (i0 + gpp) * N_KEYS)
                ht_all = jnp.dot(u_ref[prow, :], xt_ref[:, cols], preferred_element_type=F32)
                for g in range(gpp):
                    rows = slice((i0 + g) * N_KEYS, (i0 + g + 1) * N_KEYS)
                    i = e * n_grp + i0 + g
                    ht = ht_all[g * N_KEYS:(g + 1) * N_KEYS]
                    act = (0.5 * ht * (1.0 + lax.erf(ht * (2.0 ** -0.5)))).astype(BF16)
                    gate = jnp.zeros((N_KEYS, tw), BF16)
                    for h in range(PEER_HEADS):
                        lam_b = jnp.broadcast_to(
                            lam_ref[h, pl.ds(i, 1), :][:, cols].astype(BF16), (N_KEYS, tw))
                        p_b = jnp.broadcast_to(
                            pw_ref[h, pl.ds(i, 1), :][:, cols].astype(BF16), (N_KEYS, tw))
                        sel = jnp.where(r2_ref[h, :, cols] < lam_b, qw_ref[h, :, cols],
                                        jnp.zeros_like(p_b))
                        gate = gate + sel * p_b
                    hid_scr[rows, cols] = act * gate
            pieces.append(piece)
        return pieces

    def value_pieces(par):
        hid_scr = hid_bufs[par]
        k_tiles = te // MXU_TILE
        pieces = []
        for nt in range(tb // MXU_TILE):
            cols = slice(nt * MXU_TILE, (nt + 1) * MXU_TILE)
            state = {}
            for kt in range(k_tiles):
                ks = slice(kt * MXU_TILE, (kt + 1) * MXU_TILE)

                def piece(ks=ks, cols=cols, kt=kt, state=state):
                    part = jnp.dot(vt_ref[:, ks], hid_scr[ks, cols], preferred_element_type=F32)
                    state["sum"] = part if kt == 0 else state["sum"] + part
                    if kt == k_tiles - 1:
                        acc_scr[:, cols] += state.pop("sum")
                pieces.append(piece)
        return pieces

    def run(*stages):
        order = sorted(((k + 0.5) / len(st), s_idx, k)
                       for s_idx, st in enumerate(stages) for k in range(len(st)))
        for _, s_idx, k in order:
            stages[s_idx][k]()

    @pl.when(e == 0)
    def _():
        acc_scr[...] = jnp.zeros(acc_scr.shape, F32)
        run(hidden_pieces(0))

    for par in (0, 1):
        @pl.when((e >= 1) & (e < n_e) & (e % 2 == par))
        def _(par=par):
            run(value_pieces(1 - par), hidden_pieces(par))

    @pl.when(e == n_e)
    def _():
        run(value_pieces(1 - n_e % 2))
        o_ref[...] = _rms(h_ref[...] + acc_scr[...].T, fw_ref[...])


def _peer(xt, u_bf, vt_bf, lam, pw, r2, qw, h1, fw, tb, te):
    n = xt.shape[1]
    n_e = N_EXPERTS // te
    assert n_e >= 2
    kernel = functools.partial(_peer_kernel, te=te, n_e=n_e)
    rt = pl.BlockSpec((PEER_HEADS, N_KEYS, tb), lambda t, e: (0, 0, t))
    return pl.pallas_call(
        kernel,
        grid=(n // tb, n_e + 1),
        in_specs=[pl.BlockSpec((D_MODEL, tb), lambda t, e: (0, t)),
                  pl.BlockSpec((te, D_MODEL), lambda t, e: (jnp.minimum(e, n_e - 1), 0)),
                  pl.BlockSpec((None, D_MODEL, te), lambda t, e: (jnp.maximum(e - 1, 0), 0, 0)),
                  rt, rt, rt, rt,
                  pl.BlockSpec((tb, D_MODEL), lambda t, e: (t, 0)),
                  pl.BlockSpec((1, D_MODEL), lambda t, e: (0, 0))],
        out_specs=pl.BlockSpec((tb, D_MODEL), lambda t, e: (t, 0)),
        out_shape=jax.ShapeDtypeStruct((n, D_MODEL), F32),
        scratch_shapes=[pltpu.VMEM((D_MODEL, tb), F32),
                        pltpu.VMEM((te, tb), BF16), pltpu.VMEM((te, tb), BF16)],
        compiler_params=pltpu.CompilerParams(
            dimension_semantics=("arbitrary", "arbitrary"),
            vmem_limit_bytes=VMEM_LIMIT),
        name="peer_experts",
    )(xt, u_bf, vt_bf, lam, pw, r2, qw, h1, fw)


def _rope_tables(T):
    d = ATT_HEAD_DIM
    inv_freq = ROPE_THETA ** (-jnp.arange(0, d, 2, dtype=F32) / d)
    ang = jnp.arange(T, dtype=F32)[:, None] * inv_freq[None, :]
    ang = jnp.concatenate([ang, ang], axis=-1)
    sign = jnp.where(jnp.arange(d) < d // 2, -1.0, 1.0).astype(F32)
    cos = jnp.tile(jnp.cos(ang), (1, 2))
    sin = jnp.tile(jnp.sin(ang) * sign[None, :], (1, 2))
    return cos, sin


def _pick(n, prefs):
    for p in prefs:
        if n % p == 0:
            return p
    raise ValueError(f"no supported tile for extent {n}")


def kernel(x, meta_tokens, mix_norm_w, w_in, rec_lb_logits, rec_norm_w, diff_lambda_q1, diff_lambda_k1, diff_lambda_q2, diff_lambda_k2, diff_subln_w, w_out, ffn_norm_w, peer_w_query, peer_subkeys, peer_u, peer_v, final_norm_w):
    B, S, D = x.shape
    assert D == D_MODEL and w_in.shape[0] == 1 and S % REC_CHUNK == 0
    n = B * S
    x2 = x.reshape(n, D)

    cos, sin = _rope_tables(N_META + S)
    w_bf = w_in[0].astype(BF16)
    nw = mix_norm_w[0].reshape(1, D)

    tm = _pick(S, (512, 256, 128))
    q, k, v, rq, rf, ri, rg = _inproj(x2, nw, w_bf, cos[N_META:], sin[N_META:], tm)
    mq_a, mk_a, mv_a, mrq, mrf, mri, _ = _inproj(
        meta_tokens.astype(F32), nw, w_bf, cos[:N_META], sin[:N_META], N_META)

    bq = _pick(S, (512, 256, 128))
    pad_m = lambda z: jnp.pad(z, ((0, HEAD_W - N_META), (0, 0)))
    lam4 = jnp.stack([diff_lambda_q1[0], diff_lambda_k1[0],
                      diff_lambda_q2[0], diff_lambda_k2[0]]).astype(F32)
    sw = diff_subln_w[0].reshape(1, HEAD_W)
    r3 = lambda z: z.reshape(B, S, GROUP_W)
    att = _attention(lam4, r3(q), r3(k), r3(v), pad_m(mk_a), pad_m(mv_a), sw, bq)

    lb = jax.nn.softmax(rec_lb_logits.astype(F32), axis=0)[0].reshape(1, GROUP_W)
    front = lambda z: jnp.pad(z, ((REC_CHUNK - N_META, 0), (0, 0)))
    rec = _hgrn(jnp.asarray(_cum_matrix(), BF16), lb, rec_norm_w[0].reshape(1, GROUP_W),
                r3(rq), r3(rf), r3(ri), r3(rg), front(mrq), front(mrf), front(mri))

    wqt = peer_w_query[0].T.astype(BF16)
    sk = peer_subkeys[0].reshape(2 * PEER_HEADS, N_KEYS, -1).astype(BF16)
    h1, xt, st = _outproj(x2, att.reshape(n, GROUP_W), rec.reshape(n, GROUP_W),
                          w_out[0].astype(BF16), ffn_norm_w[0].reshape(1, D), wqt, sk, tm)

    lam, pw, r2, qw = _route(st, _pick(n, (256, 128)))

    tb = _pick(n, (512, 256, 128))
    te = PEER_EXPERT_TILE
    vt_tiles = peer_v[0].astype(BF16).reshape(N_EXPERTS // te, te, D).transpose(0, 2, 1)
    out = _peer(xt, peer_u[0].astype(BF16), vt_tiles, lam, pw, r2, qw,
                h1, final_norm_w.reshape(1, D), tb, te)
    return out.reshape(B, S, D)
```

```python
import functools
import itertools
import math

import numpy as np
import jax
import jax.numpy as jnp
from jax import lax
from jax.experimental import pallas as pl
from jax.experimental.pallas import tpu as pltpu

F32 = jnp.float32
BF16 = jnp.bfloat16

D_MODEL = 1024
N_META = 16
ATT_HEADS = 4
ATT_HEAD_DIM = 64
REC_HEADS = 4
HEAD_W = 128
GROUP_W = 512
N_GROUPS = 7
ROPE_THETA = 10000.0
PEER_HEADS = 8
N_KEYS = 128
N_EXPERTS = N_KEYS * N_KEYS
PEER_TOPK = 16
EPS = 1e-6
LAM_INIT = 0.8 - 0.6 * math.exp(-0.3 * 0)
NEG = -1e30
REC_CHUNK = 128
PEER_EXPERT_TILE = 2048
PEER_TOKEN_STRIP = 256
PEER_HIDDEN_ROWS = 128
MXU_TILE = 256
VMEM_LIMIT = 56 * 1024 * 1024

_NT = (((1,), (1,)), ((), ()))


def _rms(x, w):
    return x * lax.rsqrt(jnp.mean(x * x, axis=-1, keepdims=True) + EPS) * w


def _inproj_kernel(x_ref, nw_ref, w_ref, cos_ref, sin_ref,
                   q_ref, k_ref, v_ref, rq_ref, rf_ref, ri_ref, rg_ref):
    a = _rms(x_ref[...], nw_ref[...]).astype(BF16)
    cos = cos_ref[...]
    sin = sin_ref[...]
    lane = lax.broadcasted_iota(jnp.int32, (1, HEAD_W), 1)
    lo = (lane % ATT_HEAD_DIM) < (ATT_HEAD_DIM // 2)

    def proj(c):
        return jnp.dot(a, w_ref[:, c * GROUP_W:(c + 1) * GROUP_W],
                       preferred_element_type=F32)

    def rope(y, scale):
        outs = []
        for h in range(ATT_HEADS):
            z = y[:, h * HEAD_W:(h + 1) * HEAD_W]
            rot = jnp.where(lo, pltpu.roll(z, HEAD_W - 32, 1), pltpu.roll(z, 32, 1))
            r = z * cos + rot * sin
            if scale is not None:
                r = r * scale
            outs.append(r)
        return jnp.concatenate(outs, axis=1)

    q_ref[...] = rope(proj(0), ATT_HEAD_DIM ** -0.5 * math.log2(math.e)).astype(q_ref.dtype)
    k_ref[...] = rope(proj(1), None).astype(k_ref.dtype)
    v_ref[...] = proj(2).astype(v_ref.dtype)
    rq_ref[...] = proj(3).astype(rq_ref.dtype)
    rf_ref[...] = proj(4).astype(rf_ref.dtype)
    ri_ref[...] = proj(5).astype(ri_ref.dtype)
    rg_ref[...] = proj(6).astype(rg_ref.dtype)


def _inproj(x2, nw, w_bf, cos_t, sin_t, tm):
    n = x2.shape[0]
    n_pos = cos_t.shape[0] // tm
    row = lambda i: (i, 0)
    fixed = lambda i: (0, 0)
    pos = lambda i: (i % n_pos, 0)
    dts = (BF16, BF16, BF16, BF16, F32, BF16, BF16)
    return pl.pallas_call(
        _inproj_kernel,
        grid=(n // tm,),
        in_specs=[pl.BlockSpec((tm, D_MODEL), row),
                  pl.BlockSpec((1, D_MODEL), fixed),
                  pl.BlockSpec((D_MODEL, N_GROUPS * GROUP_W), fixed),
                  pl.BlockSpec((tm, HEAD_W), pos),
                  pl.BlockSpec((tm, HEAD_W), pos)],
        out_specs=[pl.BlockSpec((tm, GROUP_W), row)] * N_GROUPS,
        out_shape=[jax.ShapeDtypeStruct((n, GROUP_W), dt) for dt in dts],
        compiler_params=pltpu.CompilerParams(
            dimension_semantics=("arbitrary",), vmem_limit_bytes=VMEM_LIMIT),
        name="inproj",
    )(x2, nw, w_bf, cos_t, sin_t)


def _attn_kernel(lam_ref, q_ref, k_ref, v_ref, km_ref, vm_ref, sw_ref, o_ref,
                 m_scr, acc_scr, s0_scr, s1_scr, *, bq):
    qi = pl.program_id(2)
    q = q_ref[...]
    lane = lax.broadcasted_iota(jnp.int32, (1, HEAD_W), 1)
    zero = jnp.zeros_like(q)
    qq = jnp.concatenate([jnp.where(lane < ATT_HEAD_DIM, q, zero),
                          jnp.where(lane >= ATT_HEAD_DIM, q, zero)], axis=0)

    m_scr[...] = jnp.full(m_scr.shape, NEG, F32)
    acc_scr[...] = jnp.zeros(acc_scr.shape, F32)

    def kv(ref, j):
        return ref[pl.ds(pl.multiple_of(j * bq, bq), bq), :]

    def scores(kb):
        return lax.dot_general(qq, kb, _NT, preferred_element_type=F32)

    def accumulate(s, vb, mask):
        if mask is not None:
            s = jnp.where(mask, s, NEG)
        m_prev = m_scr[...]
        m_new = jnp.maximum(m_prev, jnp.max(s, axis=1, keepdims=True))
        alpha = jnp.exp2(m_prev - m_new)
        p = jnp.exp2(s - jnp.concatenate([m_new] * (s.shape[1] // HEAD_W), axis=1))
        v_ext = jnp.concatenate([vb, jnp.ones_like(vb)], axis=1)
        acc_scr[...] = (jnp.concatenate([alpha, alpha], axis=1) * acc_scr[...]
                        + jnp.dot(p.astype(BF16), v_ext, preferred_element_type=F32))
        m_scr[...] = m_new

    col_m = lax.broadcasted_iota(jnp.int32, (1, km_ref.shape[0]), 1)
    accumulate(scores(km_ref[...]), vm_ref[...], col_m < N_META)

    s0_scr[...] = scores(kv(k_ref, 0))

    def body(jj, carry):
        j = 2 * jj
        s1_scr[...] = scores(kv(k_ref, j + 1))
        accumulate(s0_scr[...], kv(v_ref, j), None)
        s0_scr[...] = scores(kv(k_ref, j + 2))
        accumulate(s1_scr[...], kv(v_ref, j + 1), None)
        return carry

    lax.fori_loop(0, qi // 2, body, 0)

    row = lax.broadcasted_iota(jnp.int32, (2 * bq, bq), 0) % bq
    col = lax.broadcasted_iota(jnp.int32, (2 * bq, bq), 1)
    causal = col <= row

    @pl.when(qi % 2 == 0)
    def _():
        accumulate(s0_scr[...], kv(v_ref, qi), causal)

    @pl.when(qi % 2 == 1)
    def _():
        s1_scr[...] = scores(kv(k_ref, qi))
        accumulate(s0_scr[...], kv(v_ref, qi - 1), None)
        accumulate(s1_scr[...], kv(v_ref, qi), causal)

    lv = lam_ref[...]
    lam = (jnp.exp(jnp.sum(lv[0:1] * lv[1:2], axis=1, keepdims=True))
           - jnp.exp(jnp.sum(lv[2:3] * lv[3:4], axis=1, keepdims=True)) + LAM_INIT)
    acc = acc_scr[...]
    num = acc[:, :HEAD_W]
    den = acc[:, HEAD_W:]
    o = num[:bq] / den[:bq] - lam * (num[bq:] / den[bq:])
    o_ref[...] = (_rms(o, sw_ref[...]) * (1.0 - LAM_INIT)).astype(o_ref.dtype)


def _attention(lam4, q, k, v, km, vm, sw, bq):
    B, S, _ = q.shape
    kernel = functools.partial(_attn_kernel, bq=bq)
    return pl.pallas_call(
        kernel,
        grid=(B, ATT_HEADS, S // bq),
        in_specs=[pl.BlockSpec((4, ATT_HEAD_DIM), lambda b, h, i: (0, 0)),
                  pl.BlockSpec((None, bq, HEAD_W), lambda b, h, i: (b, i, h)),
                  pl.BlockSpec((None, S, HEAD_W), lambda b, h, i: (b, 0, h)),
                  pl.BlockSpec((None, S, HEAD_W), lambda b, h, i: (b, 0, h)),
                  pl.BlockSpec((km.shape[0], HEAD_W), lambda b, h, i: (0, h)),
                  pl.BlockSpec((km.shape[0], HEAD_W), lambda b, h, i: (0, h)),
                  pl.BlockSpec((1, HEAD_W), lambda b, h, i: (0, 0))],
        out_specs=pl.BlockSpec((None, bq, HEAD_W), lambda b, h, i: (b, i, h)),
        out_shape=jax.ShapeDtypeStruct((B, S, GROUP_W), BF16),
        scratch_shapes=[pltpu.VMEM((2 * bq, HEAD_W), F32),
                        pltpu.VMEM((2 * bq, 2 * HEAD_W), F32),
                        pltpu.VMEM((2 * bq, bq), F32),
                        pltpu.VMEM((2 * bq, bq), F32)],
        compiler_params=pltpu.CompilerParams(
            dimension_semantics=("arbitrary", "arbitrary", "arbitrary"),
            vmem_limit_bytes=VMEM_LIMIT),
        name="diff_attention",
    )(lam4, q, k, v, km, vm, sw)


def _cum_matrix():
    r = np.arange(REC_CHUNK)
    return (r[None, :] <= r[:, None]).astype(np.float32)


def _block_rows(cum, period, offset):
    return jnp.concatenate(
        [jnp.broadcast_to(cum[b * period + offset:b * period + offset + 1], (period, cum.shape[1]))
         for b in range(cum.shape[0] // period)], axis=0)


def _gates(rq, rf, lb):
    q = rq * jax.nn.sigmoid(rq)
    f = lb + (1.0 - lb) * jax.nn.sigmoid(rf)
    return q, 1.0 - f, jnp.log(f)


def _hgrn_kernel(cm_ref, lb_ref, nw_ref, rq_ref, rf_ref, ri_ref, rg_ref,
                 mq_ref, mf_ref, mi_ref, o_ref, st_scr):
    c_idx = pl.program_id(1)
    C = REC_CHUNK
    tri = cm_ref[...]

    def cumsum(g):
        g0 = g.astype(BF16)
        r1 = g - g0.astype(F32)
        g1 = r1.astype(BF16)
        g2 = (r1 - g1.astype(F32)).astype(BF16)
        return (jnp.dot(tri, g0, preferred_element_type=F32)
                + jnp.dot(tri, g1, preferred_element_type=F32)
                + jnp.dot(tri, g2, preferred_element_type=F32))

    def state_update(st, k, v32, cum):
        last = cum[C - 1:C]
        ke = (k * jnp.exp(last - cum)).astype(BF16)
        upd = jnp.dot(v32.T.astype(BF16), ke, preferred_element_type=F32)
        return st * jnp.exp(last) + upd

    @pl.when(c_idx == 0)
    def _():
        valid = lax.broadcasted_iota(jnp.int32, (C, 1), 0) >= C - N_META
        for h in range(REC_HEADS):
            sl = slice(h * HEAD_W, (h + 1) * HEAD_W)
            _, k, g = _gates(mq_ref[:, sl].astype(F32), mf_ref[:, sl], lb_ref[:, sl])
            k = jnp.where(valid, k, 0.0)
            g = jnp.where(valid, g, 0.0)
            v32 = jnp.where(valid, mi_ref[:, sl].astype(F32), 0.0)
            cum = cumsum(g)
            st_scr[h] = state_update(jnp.zeros((HEAD_W, HEAD_W), F32), k, v32, cum)

    t = lax.broadcasted_iota(jnp.int32, (C, C), 0)
    s = lax.broadcasted_iota(jnp.int32, (C, C), 1)
    mask_d = (t // 16 == s // 16) & (s <= t)
    masks = [(t // (2 * hf) == s // (2 * hf)) & ((t // hf) % 2 == 1) & ((s // hf) % 2 == 0)
             for hf in (16, 32, 64)]

    def head_stages(h):
        sl = slice(h * HEAD_W, (h + 1) * HEAD_W)
        q, k, g = _gates(rq_ref[:, sl].astype(F32), rf_ref[:, sl], lb_ref[:, sl])
        v32 = ri_ref[:, sl].astype(F32)
        cum = cumsum(g)
        yield
        st = st_scr[h]
        qe = (q * jnp.exp(cum)).astype(BF16)
        o = lax.dot_general(qe, st.astype(BF16), _NT, preferred_element_type=F32)
        yield

        def pair_scores(ref, clamp):
            dq = cum - ref
            dk = ref - cum
            if clamp:
                dq = jnp.minimum(dq, 0.0)
                dk = jnp.minimum(dk, 0.0)
            return lax.dot_general((q * jnp.exp(dq)).astype(BF16),
                                   (k * jnp.exp(dk)).astype(BF16), _NT,
                                   preferred_element_type=F32)

        a = jnp.where(mask_d, pair_scores(_block_rows(cum, 16, 7), False), 0.0)
        yield
        for hf, mk in zip((16, 32, 64), masks):
            a = jnp.where(mk, pair_scores(_block_rows(cum, 2 * hf, hf - 1), True), a)
            yield
        o = o + jnp.dot(a.astype(BF16), v32.astype(BF16), preferred_element_type=F32)
        st_scr[h] = state_update(st, k, v32, cum)
        yield
        rg = rg_ref[:, sl].astype(F32)
        o = _rms(o, nw_ref[:, sl]) * (rg * jax.nn.sigmoid(rg))
        o_ref[:, sl] = o.astype(o_ref.dtype)

    for _ in itertools.zip_longest(*[head_stages(h) for h in range(REC_HEADS)]):
        pass


def _hgrn(cm, lb, nw, rq, rf, ri, rg, mq, mf, mi):
    B, S, _ = rq.shape
    C = REC_CHUNK
    tok = pl.BlockSpec((None, C, GROUP_W), lambda b, c: (b, c, 0))
    fix = lambda shape: pl.BlockSpec(shape, lambda b, c: (0, 0))
    return pl.pallas_call(
        _hgrn_kernel,
        grid=(B, S // C),
        in_specs=[fix(cm.shape), fix((1, GROUP_W)), fix((1, GROUP_W)),
                  tok, tok, tok, tok,
                  fix((C, GROUP_W)), fix((C, GROUP_W)), fix((C, GROUP_W))],
        out_specs=tok,
        out_shape=jax.ShapeDtypeStruct((B, S, GROUP_W), BF16),
        scratch_shapes=[pltpu.VMEM((REC_HEADS, HEAD_W, HEAD_W), F32)],
        compiler_params=pltpu.CompilerParams(
            dimension_semantics=("arbitrary", "arbitrary"),
            vmem_limit_bytes=VMEM_LIMIT),
        name="hgrn2",
    )(cm, lb, nw, rq, rf, ri, rg, mq, mf, mi)


def _outproj_kernel(x_ref, att_ref, rec_ref, wo_ref, fw_ref, wq_ref, sk_ref,
                    h_ref, xt_ref, st_ref):
    h1 = (x_ref[...]
          + jnp.dot(att_ref[...], wo_ref[:GROUP_W, :], preferred_element_type=F32)
          + jnp.dot(rec_ref[...], wo_ref[GROUP_W:, :], preferred_element_type=F32))
    h_ref[...] = h1
    xt = _rms(h1, fw_ref[...]).T.astype(BF16)
    xt_ref[...] = xt
    qt = jnp.dot(wq_ref[...], xt, preferred_element_type=F32).astype(BF16)
    for g in range(2 * PEER_HEADS):
        sl = slice(g * N_KEYS, (g + 1) * N_KEYS)
        st_ref[sl, :] = jnp.dot(sk_ref[g], qt[sl, :], preferred_element_type=F32)


def _outproj(x2, att, rec, wo, fw, wqt, sk, tm):
    n = x2.shape[0]
    row = lambda i: (i, 0)
    colb = lambda i: (0, i)
    fixed = lambda i: (0, 0)
    return pl.pallas_call(
        _outproj_kernel,
        grid=(n // tm,),
        in_specs=[pl.BlockSpec((tm, D_MODEL), row),
                  pl.BlockSpec((tm, GROUP_W), row),
                  pl.BlockSpec((tm, GROUP_W), row),
                  pl.BlockSpec((D_MODEL, D_MODEL), fixed),
                  pl.BlockSpec((1, D_MODEL), fixed),
                  pl.BlockSpec(wqt.shape, fixed),
                  pl.BlockSpec(sk.shape, lambda i: (0, 0, 0))],
        out_specs=[pl.BlockSpec((tm, D_MODEL), row),
                   pl.BlockSpec((D_MODEL, tm), colb),
                   pl.BlockSpec((wqt.shape[0], tm), colb)],
        out_shape=[jax.ShapeDtypeStruct((n, D_MODEL), F32),
                   jax.ShapeDtypeStruct((D_MODEL, n), BF16),
                   jax.ShapeDtypeStruct((wqt.shape[0], n), F32)],
        compiler_params=pltpu.CompilerParams(
            dimension_semantics=("arbitrary",), vmem_limit_bytes=VMEM_LIMIT),
        name="outproj_scores",
    )(x2, att, rec, wo, fw, wqt, sk)


def _staircase():
    return [(p, q) for p in range(PEER_TOPK) for q in range(PEER_TOPK)
            if (p + 1) * (q + 1) <= PEER_TOPK]


def _sort_network(n):
    pairs = []
    p = 1
    while p < n:
        k = p
        while k >= 1:
            for j in range(k % p, n - k, 2 * k):
                for i in range(min(k, n - j - k)):
                    if (i + j) // (2 * p) == (i + j + k) // (2 * p):
                        pairs.append((i + j, i + j + k))
            k //= 2
        p *= 2
    return pairs


def _sorted_desc(xs):
    xs = list(xs)
    for i, j in _sort_network(len(xs)):
        xs[i], xs[j] = jnp.maximum(xs[i], xs[j]), jnp.minimum(xs[i], xs[j])
    return xs


def _merge_top(a, b):
    n = len(a)
    xs = [jnp.maximum(a[i], b[n - 1 - i]) for i in range(n)]
    d = n // 2
    while d >= 1:
        for i in range(n):
            if (i // d) % 2 == 0:
                xs[i], xs[i + d] = jnp.maximum(xs[i], xs[i + d]), jnp.minimum(xs[i], xs[i + d])
        d //= 2
    return xs


def _prefix_count(tops, test):
    assert len(tops) == 16

    def pick(bits, lo):
        step = 8
        idx = [lo]
        for _ in bits:
            idx = [i + d for i in idx for d in (0, step)]
            step //= 2
        vals = [tops[i] for i in idx]
        for b in reversed(bits):
            vals = [jnp.where(b, vals[2 * k + 1], vals[2 * k]) for k in range(len(vals) // 2)]
        return vals[0]

    t1 = test(tops[7])
    t2 = test(pick([t1], 3))
    t3 = test(pick([t1, t2], 1))
    t4 = test(pick([t1, t2, t3], 0))
    t5 = test(tops[15])
    one = lambda t, w: jnp.where(t, float(w), 0.0)
    return one(t1, 8) + one(t2, 4) + one(t3, 2) + one(t4, 1) + one(t5, 1)


def _route_kernel(st_ref, lam_ref, pw_ref, r2_ref, qw_ref, thr_scr, top_scr):
    K = PEER_TOPK
    SUB = 8
    tb = st_ref.shape[1]
    pairs = _staircase()

    def key_slab(g, v):
        return st_ref[g * N_KEYS + v * SUB:g * N_KEYS + (v + 1) * SUB, :]

    for g in range(2 * PEER_HEADS):
        srt = _sorted_desc([key_slab(g, v) for v in range(N_KEYS // SUB)])
        for shift in (4, 2, 1):
            srt = _merge_top(srt, [pltpu.roll(x, shift, 0) for x in srt])
        for p in range(K):
            top_scr[g, p] = srt[p]

    class _Lazy:
        def __init__(self, g):
            self.g = g

        def __getitem__(self, p):
            return top_scr[self.g, p]

        def __len__(self):
            return K

    tops = [_Lazy(g) for g in range(2 * PEER_HEADS)]

    sub = lax.broadcasted_iota(jnp.int32, (PEER_HEADS, tb), 0)

    def stack_heads(part, p):
        out = tops[part][p]
        for h in range(1, PEER_HEADS):
            out = jnp.where(sub == h, tops[2 * h + part][p], out)
        return out

    a = [stack_heads(0, p) for p in range(K)]
    b = [stack_heads(1, p) for p in range(K)]
    sums = [a[p] + b[q] for (p, q) in pairs]
    n_full = len(sums) // K * K
    top = None
    for k in range(0, n_full, K):
        run = _sorted_desc(sums[k:k + K])
        top = run if top is None else _merge_top(top, run)
    for extra in sums[n_full:]:
        top = [jnp.maximum(top[0], extra)] + [
            jnp.maximum(top[i], jnp.minimum(top[i - 1], extra)) for i in range(1, K)]
    thr_scr[...] = top[K - 1]
    thr = thr_scr[...]
    m0 = sums[0]
    z = jnp.zeros((PEER_HEADS, tb), F32)
    for v in sums:
        z = z + jnp.where(v >= thr, jnp.exp(v - m0), 0.0)
    inv_z = 1.0 / z

    for h in range(PEER_HEADS):
        top1, top2 = tops[2 * h], tops[2 * h + 1]
        thr_h = jnp.broadcast_to(thr[h:h + 1, :], (SUB, tb))
        inv_z_h = jnp.broadcast_to(inv_z[h:h + 1, :], (SUB, tb))
        for v in range(N_KEYS // SUB):
            rows = slice(v * SUB, (v + 1) * SUB)
            s1 = key_slab(2 * h, v)
            lam_ref[h, rows, :] = _prefix_count(top2, lambda bq: s1 + bq >= thr_h)
            pw_ref[h, rows, :] = jnp.exp(s1 - top1[0]) * inv_z_h
        for v in range(0, N_KEYS // SUB, 2):
            rows = slice(v * SUB, (v + 2) * SUB)
            s2 = jnp.concatenate([key_slab(2 * h + 1, v), key_slab(2 * h + 1, v + 1)], axis=0)
            top2_w = [jnp.concatenate([top2[p], top2[p]], axis=0) for p in range(K)]
            rank = _prefix_count(top2_w, lambda bp: s2 < bp)
            r2_ref[h, rows, :] = rank.astype(r2_ref.dtype)
            qw_ref[h, rows, :] = jnp.exp(s2 - top2_w[0]).astype(qw_ref.dtype)


def _route(st, tb):
    n = st.shape[1]
    spec = pl.BlockSpec((PEER_HEADS, N_KEYS, tb), lambda i: (0, 0, i))
    shp = lambda dt: jax.ShapeDtypeStruct((PEER_HEADS, N_KEYS, n), dt)
    return pl.pallas_call(
        _route_kernel,
        grid=(n // tb,),
        in_specs=[pl.BlockSpec((st.shape[0], tb), lambda i: (0, i))],
        out_specs=[spec, spec, spec, spec],
        out_shape=[shp(F32), shp(F32), shp(BF16), shp(BF16)],
        scratch_shapes=[pltpu.VMEM((PEER_HEADS, tb), F32),
                        pltpu.VMEM((2 * PEER_HEADS, PEER_TOPK, 8, tb), F32)],
        compiler_params=pltpu.CompilerParams(
            dimension_semantics=("arbitrary",), vmem_limit_bytes=VMEM_LIMIT),
        name="peer_route",
    )(st)


def _peer_kernel(xt_ref, u_ref, vt_ref, lam_ref, pw_ref, r2_ref, qw_ref,
                 h_ref, fw_ref, o_ref, acc_scr, hid0_scr, hid1_scr, *, te, n_e):
    e = pl.program_id(1)
    tb = xt_ref.shape[1]
    n_grp = te // N_KEYS
    hid_bufs = (hid0_scr, hid1_scr)

    def hidden_pieces(par):
        hid_scr = hid_bufs[par]
        pieces = []
        tw = min(tb, PEER_TOKEN_STRIP)
        gpp = PEER_HIDDEN_ROWS // N_KEYS
        for i0, ct in itertools.product(range(0, n_grp, gpp), range(tb // tw)):
            def piece(i0=i0, ct=ct):
                cols = slice(ct * tw, (ct + 1) * tw)
                prow = slice(i0 * N_KEYS, (i0 + gpp) * N_KEYS)
                ht_all = jnp.dot(u_ref[prow, :], xt_ref[:, cols], preferred_element_type=F32)
                for g in range(gpp):
                    rows = slice((i0 + g) * N_KEYS, (i0 + g + 1) * N_KEYS)
                    i = e * n_grp + i0 + g
                    ht = ht_all[g * N_KEYS:(g + 1) * N_KEYS]
                    act = (0.5 * ht * (1.0 + lax.erf(ht * (2.0 ** -0.5)))).astype(BF16)
                    gate = jnp.zeros((N_KEYS, tw), BF16)
                    for h in range(PEER_HEADS):
                        lam_b = jnp.broadcast_to(
                            lam_ref[h, pl.ds(i, 1), :][:, cols].astype(BF16), (N_KEYS, tw))
                        p_b = jnp.broadcast_to(
                            pw_ref[h, pl.ds(i, 1), :][:, cols].astype(BF16), (N_KEYS, tw))
                        sel = jnp.where(r2_ref[h, :, cols] < lam_b, qw_ref[h, :, cols],
                                        jnp.zeros_like(p_b))
                        gate = gate + sel * p_b
                    hid_scr[rows, cols] = act * gate
            pieces.append(piece)
        return pieces

    def value_pieces(par):
        hid_scr = hid_bufs[par]
        k_tiles = te // MXU_TILE
        pieces = []
        for nt in range(tb // MXU_TILE):
            cols = slice(nt * MXU_TILE, (nt + 1) * MXU_TILE)
            state = {}
            for kt in range(k_tiles):
                ks = slice(kt * MXU_TILE, (kt + 1) * MXU_TILE)

                def piece(ks=ks, cols=cols, kt=kt, state=state):
                    part = jnp.dot(vt_ref[:, ks], hid_scr[ks, cols], preferred_element_type=F32)
                    state["sum"] = part if kt == 0 else state["sum"] + part
                    if kt == k_tiles - 1:
                        acc_scr[:, cols] += state.pop("sum")
                pieces.append(piece)
        return pieces

    def run(*stages):
        order = sorted(((k + 0.5) / len(st), s_idx, k)
                       for s_idx, st in enumerate(stages) for k in range(len(st)))
        for _, s_idx, k in order:
            stages[s_idx][k]()

    @pl.when(e == 0)
    def _():
        acc_scr[...] = jnp.zeros(acc_scr.shape, F32)
        run(hidden_pieces(0))

    for par in (0, 1):
        @pl.when((e >= 1) & (e < n_e) & (e % 2 == par))
        def _(par=par):
            run(value_pieces(1 - par), hidden_pieces(par))

    @pl.when(e == n_e)
    def _():
        run(value_pieces(1 - n_e % 2))
        o_ref[...] = _rms(h_ref[...] + acc_scr[...].T, fw_ref[...])


def _peer(xt, u_bf, vt_bf, lam, pw, r2, qw, h1, fw, tb, te):
    n = xt.shape[1]
    n_e = N_EXPERTS // te
    assert n_e >= 2
    kernel = functools.partial(_peer_kernel, te=te, n_e=n_e)
    rt = pl.BlockSpec((PEER_HEADS, N_KEYS, tb), lambda t, e: (0, 0, t))
    return pl.pallas_call(
        kernel,
        grid=(n // tb, n_e + 1),
        in_specs=[pl.BlockSpec((D_MODEL, tb), lambda t, e: (0, t)),
                  pl.BlockSpec((te, D_MODEL), lambda t, e: (jnp.minimum(e, n_e - 1), 0)),
                  pl.BlockSpec((None, D_MODEL, te), lambda t, e: (jnp.maximum(e - 1, 0), 0, 0)),
                  rt, rt, rt, rt,
                  pl.BlockSpec((tb, D_MODEL), lambda t, e: (t, 0)),
                  pl.BlockSpec((1, D_MODEL), lambda t, e: (0, 0))],
        out_specs=pl.BlockSpec((tb, D_MODEL), lambda t, e: (t, 0)),
        out_shape=jax.ShapeDtypeStruct((n, D_MODEL), F32),
        scratch_shapes=[pltpu.VMEM((D_MODEL, tb), F32),
                        pltpu.VMEM((te, tb), BF16), pltpu.VMEM((te, tb), BF16)],
        compiler_params=pltpu.CompilerParams(
            dimension_semantics=("arbitrary", "arbitrary"),
            vmem_limit_bytes=VMEM_LIMIT),
        name="peer_experts",
    )(xt, u_bf, vt_bf, lam, pw, r2, qw, h1, fw)


def _rope_tables(T):
    d = ATT_HEAD_DIM
    inv_freq = ROPE_THETA ** (-jnp.arange(0, d, 2, dtype=F32) / d)
    ang = jnp.arange(T, dtype=F32)[:, None] * inv_freq[None, :]
    ang = jnp.concatenate([ang, ang], axis=-1)
    sign = jnp.where(jnp.arange(d) < d // 2, -1.0, 1.0).astype(F32)
    cos = jnp.tile(jnp.cos(ang), (1, 2))
    sin = jnp.tile(jnp.sin(ang) * sign[None, :], (1, 2))
    return cos, sin


def _pick(n, prefs):
    for p in prefs:
        if n % p == 0:
            return p
    raise ValueError(f"no supported tile for extent {n}")


def kernel(x, meta_tokens, mix_norm_w, w_in, rec_lb_logits, rec_norm_w, diff_lambda_q1, diff_lambda_k1, diff_lambda_q2, diff_lambda_k2, diff_subln_w, w_out, ffn_norm_w, peer_w_query, peer_subkeys, peer_u, peer_v, final_norm_w):
    B, S, D = x.shape
    assert D == D_MODEL and w_in.shape[0] == 1 and S % REC_CHUNK == 0
    n = B * S
    x2 = x.reshape(n, D)

    cos, sin = _rope_tables(N_META + S)
    w_bf = w_in[0].astype(BF16)
    nw = mix_norm_w[0].reshape(1, D)

    tm = _pick(S, (512, 256, 128))
    q, k, v, rq, rf, ri, rg = _inproj(x2, nw, w_bf, cos[N_META:], sin[N_META:], tm)
    mq_a, mk_a, mv_a, mrq, mrf, mri, _ = _inproj(
        meta_tokens.astype(F32), nw, w_bf, cos[:N_META], sin[:N_META], N_META)

    bq = _pick(S, (512, 256, 128))
    pad_m = lambda z: jnp.pad(z, ((0, HEAD_W - N_META), (0, 0)))
    lam4 = jnp.stack([diff_lambda_q1[0], diff_lambda_k1[0],
                      diff_lambda_q2[0], diff_lambda_k2[0]]).astype(F32)
    sw = diff_subln_w[0].reshape(1, HEAD_W)
    r3 = lambda z: z.reshape(B, S, GROUP_W)
    att = _attention(lam4, r3(q), r3(k), r3(v), pad_m(mk_a), pad_m(mv_a), sw, bq)

    lb = jax.nn.softmax(rec_lb_logits.astype(F32), axis=0)[0].reshape(1, GROUP_W)
    front = lambda z: jnp.pad(z, ((REC_CHUNK - N_META, 0), (0, 0)))
    rec = _hgrn(jnp.asarray(_cum_matrix(), BF16), lb, rec_norm_w[0].reshape(1, GROUP_W),
                r3(rq), r3(rf), r3(ri), r3(rg), front(mrq), front(mrf), front(mri))

    wqt = peer_w_query[0].T.astype(BF16)
    sk = peer_subkeys[0].reshape(2 * PEER_HEADS, N_KEYS, -1).astype(BF16)
    h1, xt, st = _outproj(x2, att.reshape(n, GROUP_W), rec.reshape(n, GROUP_W),
                          w_out[0].astype(BF16), ffn_norm_w[0].reshape(1, D), wqt, sk, tm)

    lam, pw, r2, qw = _route(st, _pick(n, (256, 128)))

    tb = _pick(n, (512, 256, 128))
    te = PEER_EXPERT_TILE
    vt_tiles = peer_v[0].astype(BF16).reshape(N_EXPERTS // te, te, D).transpose(0, 2, 1)
    out = _peer(xt, peer_u[0].astype(BF16), vt_tiles, lam, pw, r2, qw,
                h1, final_norm_w.reshape(1, D), tb, te)
    return out.reshape(B, S, D)
```

```python
import functools
import itertools
import math

import numpy as np
import jax
import jax.numpy as jnp
from jax import lax
from jax.experimental import pallas as pl
from jax.experimental.pallas import tpu as pltpu

F32 = jnp.float32
BF16 = jnp.bfloat16

D_MODEL = 1024
N_META = 16
ATT_HEADS = 4
ATT_HEAD_DIM = 64
REC_HEADS = 4
HEAD_W = 128
GROUP_W = 512
N_GROUPS = 7
ROPE_THETA = 10000.0
PEER_HEADS = 8
N_KEYS = 128
N_EXPERTS = N_KEYS * N_KEYS
PEER_TOPK = 16
EPS = 1e-6
LAM_INIT = 0.8 - 0.6 * math.exp(-0.3 * 0)
NEG = -1e30
REC_CHUNK = 128
ATT_HEADS_PER_STEP = 2
PEER_EXPERT_TILE = 2048
PEER_TOKEN_STRIP = 512
GELU_FOLD = 0.5 * 2.0 ** 0.5
PEER_HIDDEN_ROWS = 128
MXU_TILE = 256
VMEM_LIMIT = 56 * 1024 * 1024

_NT = (((1,), (1,)), ((), ()))


def _rms(x, w):
    return x * lax.rsqrt(jnp.mean(x * x, axis=-1, keepdims=True) + EPS) * w


def _inproj_kernel(x_ref, nw_ref, w_ref, cos_ref, sin_ref,
                   q_ref, k_ref, v_ref, rq_ref, rf_ref, ri_ref, rg_ref):
    a = _rms(x_ref[...], nw_ref[...]).astype(BF16)
    cos = cos_ref[...]
    sin = sin_ref[...]
    lane = lax.broadcasted_iota(jnp.int32, (1, HEAD_W), 1)
    lo = (lane % ATT_HEAD_DIM) < (ATT_HEAD_DIM // 2)

    def proj(c):
        return jnp.dot(a, w_ref[:, c * GROUP_W:(c + 1) * GROUP_W],
                       preferred_element_type=F32)

    def rope(y, scale):
        outs = []
        for h in range(ATT_HEADS):
            z = y[:, h * HEAD_W:(h + 1) * HEAD_W]
            rot = jnp.where(lo, pltpu.roll(z, HEAD_W - 32, 1), pltpu.roll(z, 32, 1))
            r = z * cos + rot * sin
            if scale is not None:
                r = r * scale
            outs.append(r)
        return jnp.concatenate(outs, axis=1)

    q_ref[...] = rope(proj(0), ATT_HEAD_DIM ** -0.5 * math.log2(math.e)).astype(q_ref.dtype)
    k_ref[...] = rope(proj(1), None).astype(k_ref.dtype)
    v_ref[...] = proj(2).astype(v_ref.dtype)
    rq_ref[...] = proj(3).astype(rq_ref.dtype)
    rf_ref[...] = proj(4).astype(rf_ref.dtype)
    ri_ref[...] = proj(5).astype(ri_ref.dtype)
    rg_ref[...] = proj(6).astype(rg_ref.dtype)


def _inproj(x2, nw, w_bf, cos_t, sin_t, tm):
    n = x2.shape[0]
    n_pos = cos_t.shape[0] // tm
    row = lambda i: (i, 0)
    fixed = lambda i: (0, 0)
    pos = lambda i: (i % n_pos, 0)
    dts = (BF16, BF16, BF16, BF16, F32, BF16, BF16)
    return pl.pallas_call(
        _inproj_kernel,
        grid=(n // tm,),
        in_specs=[pl.BlockSpec((tm, D_MODEL), row),
                  pl.BlockSpec((1, D_MODEL), fixed),
                  pl.BlockSpec((D_MODEL, N_GROUPS * GROUP_W), fixed),
                  pl.BlockSpec((tm, HEAD_W), pos),
                  pl.BlockSpec((tm, HEAD_W), pos)],
        out_specs=[pl.BlockSpec((tm, GROUP_W), row)] * N_GROUPS,
        out_shape=[jax.ShapeDtypeStruct((n, GROUP_W), dt) for dt in dts],
        compiler_params=pltpu.CompilerParams(
            dimension_semantics=("arbitrary",), vmem_limit_bytes=VMEM_LIMIT),
        name="inproj",
    )(x2, nw, w_bf, cos_t, sin_t)


def _attn_kernel(lam_ref, q_ref, k_ref, v_ref, km_ref, vm_ref, sw_ref, o_ref,
                 m_scr, acc_scr, s0_scr, s1_scr, *, bq, nh):
    qi = pl.program_id(2)
    heads = range(nh)
    sls = [slice(h * HEAD_W, (h + 1) * HEAD_W) for h in heads]
    lane = lax.broadcasted_iota(jnp.int32, (1, HEAD_W), 1)
    qq = []
    for h in heads:
        q = q_ref[:, sls[h]]
        zero = jnp.zeros_like(q)
        qq.append(jnp.concatenate([jnp.where(lane < ATT_HEAD_DIM, q, zero),
                                   jnp.where(lane >= ATT_HEAD_DIM, q, zero)], axis=0))

    m_scr[...] = jnp.full(m_scr.shape, NEG, F32)
    acc_scr[...] = jnp.zeros(acc_scr.shape, F32)

    def kv(ref, j, h):
        return ref[pl.ds(pl.multiple_of(j * bq, bq), bq), sls[h]]

    def scores(h, kb):
        return lax.dot_general(qq[h], kb, _NT, preferred_element_type=F32)

    def accumulate(h, s, vb, mask):
        if mask is not None:
            s = jnp.where(mask, s, NEG)
        m_prev = m_scr[h]
        m_new = jnp.maximum(m_prev, jnp.max(s, axis=1, keepdims=True))
        yield
        alpha = jnp.exp2(m_prev - m_new)
        p = jnp.exp2(s - jnp.concatenate([m_new] * (s.shape[1] // HEAD_W), axis=1))
        yield
        v_ext = jnp.concatenate([vb, jnp.ones_like(vb)], axis=1)
        acc_scr[h] = (jnp.concatenate([alpha, alpha], axis=1) * acc_scr[h]
                      + jnp.dot(p.astype(BF16), v_ext, preferred_element_type=F32))
        m_scr[h] = m_new

    def lockstep(gens):
        for _ in itertools.zip_longest(*gens):
            pass

    col_m = lax.broadcasted_iota(jnp.int32, (1, km_ref.shape[0]), 1)
    lockstep([accumulate(h, scores(h, km_ref[:, sls[h]]), vm_ref[:, sls[h]], col_m < N_META)
              for h in heads])

    for h in heads:
        s0_scr[h] = scores(h, kv(k_ref, 0, h))

    def body(jj, carry):
        j = 2 * jj
        for h in heads:
            s1_scr[h] = scores(h, kv(k_ref, j + 1, h))
        lockstep([accumulate(h, s0_scr[h], kv(v_ref, j, h), None) for h in heads])
        for h in heads:
            s0_scr[h] = scores(h, kv(k_ref, j + 2, h))
        lockstep([accumulate(h, s1_scr[h], kv(v_ref, j + 1, h), None) for h in heads])
        return carry

    lax.fori_loop(0, qi // 2, body, 0)

    row = lax.broadcasted_iota(jnp.int32, (2 * bq, bq), 0) % bq
    col = lax.broadcasted_iota(jnp.int32, (2 * bq, bq), 1)
    causal = col <= row

    @pl.when(qi % 2 == 0)
    def _():
        lockstep([accumulate(h, s0_scr[h], kv(v_ref, qi, h), causal) for h in heads])

    @pl.when(qi % 2 == 1)
    def _():
        for h in heads:
            s1_scr[h] = scores(h, kv(k_ref, qi, h))
        lockstep([accumulate(h, s0_scr[h], kv(v_ref, qi - 1, h), None) for h in heads])
        lockstep([accumulate(h, s1_scr[h], kv(v_ref, qi, h), causal) for h in heads])

    lv = lam_ref[...]
    lam = (jnp.exp(jnp.sum(lv[0:1] * lv[1:2], axis=1, keepdims=True))
           - jnp.exp(jnp.sum(lv[2:3] * lv[3:4], axis=1, keepdims=True)) + LAM_INIT)
    for h in heads:
        acc = acc_scr[h]
        num = acc[:, :HEAD_W]
        den = acc[:, HEAD_W:]
        o = num[:bq] / den[:bq] - lam * (num[bq:] / den[bq:])
        o_ref[:, sls[h]] = (_rms(o, sw_ref[...]) * (1.0 - LAM_INIT)).astype(o_ref.dtype)


def _attention(lam4, q, k, v, km, vm, sw, bq):
    B, S, _ = q.shape
    nh = ATT_HEADS_PER_STEP
    w = nh * HEAD_W
    kernel = functools.partial(_attn_kernel, bq=bq, nh=nh)
    return pl.pallas_call(
        kernel,
        grid=(B, ATT_HEADS // nh, S // bq),
        in_specs=[pl.BlockSpec((4, ATT_HEAD_DIM), lambda b, h, i: (0, 0)),
                  pl.BlockSpec((None, bq, w), lambda b, h, i: (b, i, h)),
                  pl.BlockSpec((None, S, w), lambda b, h, i: (b, 0, h)),
                  pl.BlockSpec((None, S, w), lambda b, h, i: (b, 0, h)),
                  pl.BlockSpec((km.shape[0], w), lambda b, h, i: (0, h)),
                  pl.BlockSpec((km.shape[0], w), lambda b, h, i: (0, h)),
                  pl.BlockSpec((1, HEAD_W), lambda b, h, i: (0, 0))],
        out_specs=pl.BlockSpec((None, bq, w), lambda b, h, i: (b, i, h)),
        out_shape=jax.ShapeDtypeStruct((B, S, GROUP_W), BF16),
        scratch_shapes=[pltpu.VMEM((nh, 2 * bq, HEAD_W), F32),
                        pltpu.VMEM((nh, 2 * bq, 2 * HEAD_W), F32),
                        pltpu.VMEM((nh, 2 * bq, bq), F32),
                        pltpu.VMEM((nh, 2 * bq, bq), F32)],
        compiler_params=pltpu.CompilerParams(
            dimension_semantics=("arbitrary", "arbitrary", "arbitrary"),
            vmem_limit_bytes=VMEM_LIMIT),
        name="diff_attention",
    )(lam4, q, k, v, km, vm, sw)


def _cum_matrix():
    r = np.arange(REC_CHUNK)
    return (r[None, :] <= r[:, None]).astype(np.float32)


def _block_rows(cum, period, offset):
    return jnp.concatenate(
        [jnp.broadcast_to(cum[b * period + offset:b * period + offset + 1], (period, cum.shape[1]))
         for b in range(cum.shape[0] // period)], axis=0)


def _gates(rq, rf, lb):
    q = rq * jax.nn.sigmoid(rq)
    f = lb + (1.0 - lb) * jax.nn.sigmoid(rf)
    return q, 1.0 - f, jnp.log(f)


def _hgrn_kernel(cm_ref, lb_ref, nw_ref, rq_ref, rf_ref, ri_ref, rg_ref,
                 mq_ref, mf_ref, mi_ref, o_ref, st_scr):
    c_idx = pl.program_id(1)
    C = REC_CHUNK
    tri = cm_ref[...]

    def cumsum(g):
        g0 = g.astype(BF16)
        r1 = g - g0.astype(F32)
        g1 = r1.astype(BF16)
        g2 = (r1 - g1.astype(F32)).astype(BF16)
        return (jnp.dot(tri, g0, preferred_element_type=F32)
                + jnp.dot(tri, g1, preferred_element_type=F32)
                + jnp.dot(tri, g2, preferred_element_type=F32))

    def state_update(st, k, v32, cum):
        last = cum[C - 1:C]
        ke = (k * jnp.exp(last - cum)).astype(BF16)
        upd = jnp.dot(v32.T.astype(BF16), ke, preferred_element_type=F32)
        return st * jnp.exp(last) + upd

    @pl.when(c_idx == 0)
    def _():
        valid = lax.broadcasted_iota(jnp.int32, (C, 1), 0) >= C - N_META
        for h in range(REC_HEADS):
            sl = slice(h * HEAD_W, (h + 1) * HEAD_W)
            _, k, g = _gates(mq_ref[:, sl].astype(F32), mf_ref[:, sl], lb_ref[:, sl])
            k = jnp.where(valid, k, 0.0)
            g = jnp.where(valid, g, 0.0)
            v32 = jnp.where(valid, mi_ref[:, sl].astype(F32), 0.0)
            cum = cumsum(g)
            st_scr[h] = state_update(jnp.zeros((HEAD_W, HEAD_W), F32), k, v32, cum)

    t = lax.broadcasted_iota(jnp.int32, (C, C), 0)
    s = lax.broadcasted_iota(jnp.int32, (C, C), 1)
    mask_d = (t // 16 == s // 16) & (s <= t)
    masks = [(t // (2 * hf) == s // (2 * hf)) & ((t // hf) % 2 == 1) & ((s // hf) % 2 == 0)
             for hf in (16, 32, 64)]

    def head_stages(h):
        sl = slice(h * HEAD_W, (h + 1) * HEAD_W)
        q, k, g = _gates(rq_ref[:, sl].astype(F32), rf_ref[:, sl], lb_ref[:, sl])
        v32 = ri_ref[:, sl].astype(F32)
        cum = cumsum(g)
        yield
        st = st_scr[h]
        qe = (q * jnp.exp(cum)).astype(BF16)
        o = lax.dot_general(qe, st.astype(BF16), _NT, preferred_element_type=F32)
        yield

        def pair_scores(ref, clamp):
            dq = cum - ref
            dk = ref - cum
            if clamp:
                dq = jnp.minimum(dq, 0.0)
                dk = jnp.minimum(dk, 0.0)
            return lax.dot_general((q * jnp.exp(dq)).astype(BF16),
                                   (k * jnp.exp(dk)).astype(BF16), _NT,
                                   preferred_element_type=F32)

        a = jnp.where(mask_d, pair_scores(_block_rows(cum, 16, 7), False), 0.0)
        yield
        for hf, mk in zip((16, 32, 64), masks):
            a = jnp.where(mk, pair_scores(_block_rows(cum, 2 * hf, hf - 1), True), a)
            yield
        o = o + jnp.dot(a.astype(BF16), v32.astype(BF16), preferred_element_type=F32)
        st_scr[h] = state_update(st, k, v32, cum)
        yield
        rg = rg_ref[:, sl].astype(F32)
        o = _rms(o, nw_ref[:, sl]) * (rg * jax.nn.sigmoid(rg))
        o_ref[:, sl] = o.astype(o_ref.dtype)

    for _ in itertools.zip_longest(*[head_stages(h) for h in range(REC_HEADS)]):
        pass


def _hgrn(cm, lb, nw, rq, rf, ri, rg, mq, mf, mi):
    B, S, _ = rq.shape
    C = REC_CHUNK
    tok = pl.BlockSpec((None, C, GROUP_W), lambda b, c: (b, c, 0))
    fix = lambda shape: pl.BlockSpec(shape, lambda b, c: (0, 0))
    return pl.pallas_call(
        _hgrn_kernel,
        grid=(B, S // C),
        in_specs=[fix(cm.shape), fix((1, GROUP_W)), fix((1, GROUP_W)),
                  tok, tok, tok, tok,
                  fix((C, GROUP_W)), fix((C, GROUP_W)), fix((C, GROUP_W))],
        out_specs=tok,
        out_shape=jax.ShapeDtypeStruct((B, S, GROUP_W), BF16),
        scratch_shapes=[pltpu.VMEM((REC_HEADS, HEAD_W, HEAD_W), F32)],
        compiler_params=pltpu.CompilerParams(
            dimension_semantics=("arbitrary", "arbitrary"),
            vmem_limit_bytes=VMEM_LIMIT),
        name="hgrn2",
    )(cm, lb, nw, rq, rf, ri, rg, mq, mf, mi)


def _outproj_kernel(x_ref, att_ref, rec_ref, wo_ref, fw_ref, wq_ref, sk_ref,
                    h_ref, xt_ref, st_ref):
    h1 = (x_ref[...]
          + jnp.dot(att_ref[...], wo_ref[:GROUP_W, :], preferred_element_type=F32)
          + jnp.dot(rec_ref[...], wo_ref[GROUP_W:, :], preferred_element_type=F32))
    h_ref[...] = h1
    xt = _rms(h1, fw_ref[...]).T.astype(BF16)
    xt_ref[...] = xt
    qt = jnp.dot(wq_ref[...], xt, preferred_element_type=F32).astype(BF16)
    for g in range(2 * PEER_HEADS):
        sl = slice(g * N_KEYS, (g + 1) * N_KEYS)
        st_ref[sl, :] = jnp.dot(sk_ref[g], qt[sl, :], preferred_element_type=F32)


def _outproj(x2, att, rec, wo, fw, wqt, sk, tm):
    n = x2.shape[0]
    row = lambda i: (i, 0)
    colb = lambda i: (0, i)
    fixed = lambda i: (0, 0)
    return pl.pallas_call(
        _outproj_kernel,
        grid=(n // tm,),
        in_specs=[pl.BlockSpec((tm, D_MODEL), row),
                  pl.BlockSpec((tm, GROUP_W), row),
                  pl.BlockSpec((tm, GROUP_W), row),
                  pl.BlockSpec((D_MODEL, D_MODEL), fixed),
                  pl.BlockSpec((1, D_MODEL), fixed),
                  pl.BlockSpec(wqt.shape, fixed),
                  pl.BlockSpec(sk.shape, lambda i: (0, 0, 0))],
        out_specs=[pl.BlockSpec((tm, D_MODEL), row),
                   pl.BlockSpec((D_MODEL, tm), colb),
                   pl.BlockSpec((wqt.shape[0], tm), colb)],
        out_shape=[jax.ShapeDtypeStruct((n, D_MODEL), F32),
                   jax.ShapeDtypeStruct((D_MODEL, n), BF16),
                   jax.ShapeDtypeStruct((wqt.shape[0], n), F32)],
        compiler_params=pltpu.CompilerParams(
            dimension_semantics=("arbitrary",), vmem_limit_bytes=VMEM_LIMIT),
        name="outproj_scores",
    )(x2, att, rec, wo, fw, wqt, sk)


def _staircase():
    return [(p, q) for p in range(PEER_TOPK) for q in range(PEER_TOPK)
            if (p + 1) * (q + 1) <= PEER_TOPK]


def _sort_network(n):
    pairs = []
    p = 1
    while p < n:
        k = p
        while k >= 1:
            for j in range(k % p, n - k, 2 * k):
                for i in range(min(k, n - j - k)):
                    if (i + j) // (2 * p) == (i + j + k) // (2 * p):
                        pairs.append((i + j, i + j + k))
            k //= 2
        p *= 2
    return pairs


def _sorted_desc(xs):
    xs = list(xs)
    for i, j in _sort_network(len(xs)):
        xs[i], xs[j] = jnp.maximum(xs[i], xs[j]), jnp.minimum(xs[i], xs[j])
    return xs


def _merge_top(a, b):
    n = len(a)
    xs = [jnp.maximum(a[i], b[n - 1 - i]) for i in range(n)]
    d = n // 2
    while d >= 1:
        for i in range(n):
            if (i // d) % 2 == 0:
                xs[i], xs[i + d] = jnp.maximum(xs[i], xs[i + d]), jnp.minimum(xs[i], xs[i + d])
        d //= 2
    return xs


def _prefix_count(tops, test):
    assert len(tops) == 16

    def pick(bits, lo):
        step = 8
        idx = [lo]
        for _ in bits:
            idx = [i + d for i in idx for d in (0, step)]
            step //= 2
        vals = [tops[i] for i in idx]
        for b in reversed(bits):
            vals = [jnp.where(b, vals[2 * k + 1], vals[2 * k]) for k in range(len(vals) // 2)]
        return vals[0]

    t1 = test(tops[7])
    t2 = test(pick([t1], 3))
    t3 = test(pick([t1, t2], 1))
    t4 = test(pick([t1, t2, t3], 0))
    t5 = test(tops[15])
    one = lambda t, w: jnp.where(t, float(w), 0.0)
    return one(t1, 8) + one(t2, 4) + one(t3, 2) + one(t4, 1) + one(t5, 1)


def _route_kernel(st_ref, lam_ref, pw_ref, r2_ref, qw_ref, thr_scr, top_scr):
    K = PEER_TOPK
    SUB = 8
    tb = st_ref.shape[1]
    pairs = _staircase()

    def key_slab(g, v):
        return st_ref[g * N_KEYS + v * SUB:g * N_KEYS + (v + 1) * SUB, :]

    for g in range(2 * PEER_HEADS):
        srt = _sorted_desc([key_slab(g, v) for v in range(N_KEYS // SUB)])
        for shift in (4, 2, 1):
            srt = _merge_top(srt, [pltpu.roll(x, shift, 0) for x in srt])
        for p in range(K):
            top_scr[g, p] = srt[p]

    class _Lazy:
        def __init__(self, g):
            self.g = g

        def __getitem__(self, p):
            return top_scr[self.g, p]

        def __len__(self):
            return K

    tops = [_Lazy(g) for g in range(2 * PEER_HEADS)]

    sub = lax.broadcasted_iota(jnp.int32, (PEER_HEADS, tb), 0)

    def stack_heads(part, p):
        out = tops[part][p]
        for h in range(1, PEER_HEADS):
            out = jnp.where(sub == h, tops[2 * h + part][p], out)
        return out

    a = [stack_heads(0, p) for p in range(K)]
    b = [stack_heads(1, p) for p in range(K)]
    sums = [a[p] + b[q] for (p, q) in pairs]
    n_full = len(sums) // K * K
    top = None
    for k in range(0, n_full, K):
        run = _sorted_desc(sums[k:k + K])
        top = run if top is None else _merge_top(top, run)
    for extra in sums[n_full:]:
        top = [jnp.maximum(top[0], extra)] + [
            jnp.maximum(top[i], jnp.minimum(top[i - 1], extra)) for i in range(1, K)]
    thr_scr[...] = top[K - 1]
    thr = thr_scr[...]
    m0 = sums[0]
    z = jnp.zeros((PEER_HEADS, tb), F32)
    for v in sums:
        z = z + jnp.where(v >= thr, jnp.exp(v - m0), 0.0)
    inv_z = 1.0 / z

    for h in range(PEER_HEADS):
        top1, top2 = tops[2 * h], tops[2 * h + 1]
        thr_h = jnp.broadcast_to(thr[h:h + 1, :], (SUB, tb))
        inv_z_h = jnp.broadcast_to(inv_z[h:h + 1, :], (SUB, tb))
        for v in range(N_KEYS // SUB):
            rows = slice(v * SUB, (v + 1) * SUB)
            s1 = key_slab(2 * h, v)
            lam_ref[h, rows, :] = _prefix_count(top2, lambda bq: s1 + bq >= thr_h)
            pw_ref[h, rows, :] = jnp.exp(s1 - top1[0]) * inv_z_h * GELU_FOLD
        for v in range(0, N_KEYS // SUB, 2):
            rows = slice(v * SUB, (v + 2) * SUB)
            s2 = jnp.concatenate([key_slab(2 * h + 1, v), key_slab(2 * h + 1, v + 1)], axis=0)
            top2_w = [jnp.concatenate([top2[p], top2[p]], axis=0) for p in range(K)]
            rank = _prefix_count(top2_w, lambda bp: s2 < bp)
            r2_ref[h, rows, :] = rank.astype(r2_ref.dtype)
            qw_ref[h, rows, :] = jnp.exp(s2 - top2_w[0]).astype(qw_ref.dtype)


def _route(st, tb):
    n = st.shape[1]
    spec = pl.BlockSpec((PEER_HEADS, N_KEYS, tb), lambda i: (0, 0, i))
    shp = lambda dt: jax.ShapeDtypeStruct((PEER_HEADS, N_KEYS, n), dt)
    return pl.pallas_call(
        _route_kernel,
        grid=(n // tb,),
        in_specs=[pl.BlockSpec((st.shape[0], tb), lambda i: (0, i))],
        out_specs=[spec, spec, spec, spec],
        out_shape=[shp(F32), shp(F32), shp(BF16), shp(BF16)],
        scratch_shapes=[pltpu.VMEM((PEER_HEADS, tb), F32),
                        pltpu.VMEM((2 * PEER_HEADS, PEER_TOPK, 8, tb), F32)],
        compiler_params=pltpu.CompilerParams(
            dimension_semantics=("arbitrary",), vmem_limit_bytes=VMEM_LIMIT),
        name="peer_route",
    )(st)


def _peer_kernel(xt_ref, u_ref, vt_ref, lam_ref, pw_ref, r2_ref, qw_ref,
                 h_ref, fw_ref, o_ref, acc_scr, hid0_scr, hid1_scr, *, te, n_e):
    e = pl.program_id(1)
    tb = xt_ref.shape[1]
    n_grp = te // N_KEYS
    hid_bufs = (hid0_scr, hid1_scr)

    def hidden_pieces(par):
        hid_scr = hid_bufs[par]
        pieces = []
        tw = min(tb, PEER_TOKEN_STRIP)
        gpp = PEER_HIDDEN_ROWS // N_KEYS
        for i0, ct in itertools.product(range(0, n_grp, gpp), range(tb // tw)):
            def piece(i0=i0, ct=ct):
                cols = slice(ct * tw, (ct + 1) * tw)
                prow = slice(i0 * N_KEYS, (i0 + gpp) * N_KEYS)
                ht_all = jnp.dot(u_ref[prow, :], xt_ref[:, cols], preferred_element_type=F32)
                for g in range(gpp):
                    rows = slice((i0 + g) * N_KEYS, (i0 + g + 1) * N_KEYS)
                    i = e * n_grp + i0 + g
                    ht = ht_all[g * N_KEYS:(g + 1) * N_KEYS]
                    act = (ht * (1.0 + lax.erf(ht))).astype(BF16)
                    gate = None
                    for h in range(PEER_HEADS):
                        lam_b = jnp.broadcast_to(
                            lam_ref[h, pl.ds(i, 1), :][:, cols].astype(BF16), (N_KEYS, tw))
                        p_b = jnp.broadcast_to(
                            pw_ref[h, pl.ds(i, 1), :][:, cols].astype(BF16), (N_KEYS, tw))
                        sel = jnp.where(r2_ref[h, :, cols] < lam_b, qw_ref[h, :, cols],
                                        jnp.zeros_like(p_b))
                        gate = sel * p_b if gate is None else gate + sel * p_b
                    hid_scr[rows, cols] = act * gate
            pieces.append(piece)
        return pieces

    def value_pieces(par):
        hid_scr = hid_bufs[par]
        k_tiles = te // MXU_TILE
        pieces = []
        for nt in range(tb // MXU_TILE):
            cols = slice(nt * MXU_TILE, (nt + 1) * MXU_TILE)
            state = {}
            for kt in range(k_tiles):
                ks = slice(kt * MXU_TILE, (kt + 1) * MXU_TILE)

                def piece(ks=ks, cols=cols, kt=kt, state=state):
                    part = jnp.dot(vt_ref[:, ks], hid_scr[ks, cols], preferred_element_type=F32)
                    state["sum"] = part if kt == 0 else state["sum"] + part
                    if kt == k_tiles - 1:
                        acc_scr[:, cols] += state.pop("sum")
                pieces.append(piece)
        return pieces

    def run(*stages):
        order = sorted(((k + 0.5) / len(st), s_idx, k)
                       for s_idx, st in enumerate(stages) for k in range(len(st)))
        for _, s_idx, k in order:
            stages[s_idx][k]()

    @pl.when(e == 0)
    def _():
        acc_scr[...] = jnp.zeros(acc_scr.shape, F32)
        run(hidden_pieces(0))

    for par in (0, 1):
        @pl.when((e >= 1) & (e < n_e) & (e % 2 == par))
        def _(par=par):
            run(value_pieces(1 - par), hidden_pieces(par))

    @pl.when(e == n_e)
    def _():
        run(value_pieces(1 - n_e % 2))
        o_ref[...] = _rms(h_ref[...] + acc_scr[...].T, fw_ref[...])


def _peer(xt, u_bf, vt_bf, lam, pw, r2, qw, h1, fw, tb, te):
    n = xt.shape[1]
    n_e = N_EXPERTS // te
    assert n_e >= 2
    kernel = functools.partial(_peer_kernel, te=te, n_e=n_e)
    rt = pl.BlockSpec((PEER_HEADS, N_KEYS, tb), lambda t, e: (0, 0, t))
    return pl.pallas_call(
        kernel,
        grid=(n // tb, n_e + 1),
        in_specs=[pl.BlockSpec((D_MODEL, tb), lambda t, e: (0, t)),
                  pl.BlockSpec((te, D_MODEL), lambda t, e: (jnp.minimum(e, n_e - 1), 0)),
                  pl.BlockSpec((None, D_MODEL, te), lambda t, e: (jnp.maximum(e - 1, 0), 0, 0)),
                  rt, rt, rt, rt,
                  pl.BlockSpec((tb, D_MODEL), lambda t, e: (t, 0)),
                  pl.BlockSpec((1, D_MODEL), lambda t, e: (0, 0))],
        out_specs=pl.BlockSpec((tb, D_MODEL), lambda t, e: (t, 0)),
        out_shape=jax.ShapeDtypeStruct((n, D_MODEL), F32),
        scratch_shapes=[pltpu.VMEM((D_MODEL, tb), F32),
                        pltpu.VMEM((te, tb), BF16), pltpu.VMEM((te, tb), BF16)],
        compiler_params=pltpu.CompilerParams(
            dimension_semantics=("arbitrary", "arbitrary"),
            vmem_limit_bytes=VMEM_LIMIT),
        name="peer_experts",
    )(xt, u_bf, vt_bf, lam, pw, r2, qw, h1, fw)


def _rope_tables(T):
    d = ATT_HEAD_DIM
    inv_freq = ROPE_THETA ** (-jnp.arange(0, d, 2, dtype=F32) / d)
    ang = jnp.arange(T, dtype=F32)[:, None] * inv_freq[None, :]
    ang = jnp.concatenate([ang, ang], axis=-1)
    sign = jnp.where(jnp.arange(d) < d // 2, -1.0, 1.0).astype(F32)
    cos = jnp.tile(jnp.cos(ang), (1, 2))
    sin = jnp.tile(jnp.sin(ang) * sign[None, :], (1, 2))
    return cos, sin


def _pick(n, prefs):
    for p in prefs:
        if n % p == 0:
            return p
    raise ValueError(f"no supported tile for extent {n}")


def kernel(x, meta_tokens, mix_norm_w, w_in, rec_lb_logits, rec_norm_w, diff_lambda_q1, diff_lambda_k1, diff_lambda_q2, diff_lambda_k2, diff_subln_w, w_out, ffn_norm_w, peer_w_query, peer_subkeys, peer_u, peer_v, final_norm_w):
    B, S, D = x.shape
    assert D == D_MODEL and w_in.shape[0] == 1 and S % REC_CHUNK == 0
    n = B * S
    x2 = x.reshape(n, D)

    cos, sin = _rope_tables(N_META + S)
    w_bf = w_in[0].astype(BF16)
    nw = mix_norm_w[0].reshape(1, D)

    tm = _pick(S, (512, 256, 128))
    q, k, v, rq, rf, ri, rg = _inproj(x2, nw, w_bf, cos[N_META:], sin[N_META:], tm)
    mq_a, mk_a, mv_a, mrq, mrf, mri, _ = _inproj(
        meta_tokens.astype(F32), nw, w_bf, cos[:N_META], sin[:N_META], N_META)

    bq = _pick(S, (512, 256, 128))
    pad_m = lambda z: jnp.pad(z, ((0, HEAD_W - N_META), (0, 0)))
    lam4 = jnp.stack([diff_lambda_q1[0], diff_lambda_k1[0],
                      diff_lambda_q2[0], diff_lambda_k2[0]]).astype(F32)
    sw = diff_subln_w[0].reshape(1, HEAD_W)
    r3 = lambda z: z.reshape(B, S, GROUP_W)
    att = _attention(lam4, r3(q), r3(k), r3(v), pad_m(mk_a), pad_m(mv_a), sw, bq)

    lb = jax.nn.softmax(rec_lb_logits.astype(F32), axis=0)[0].reshape(1, GROUP_W)
    front = lambda z: jnp.pad(z, ((REC_CHUNK - N_META, 0), (0, 0)))
    rec = _hgrn(jnp.asarray(_cum_matrix(), BF16), lb, rec_norm_w[0].reshape(1, GROUP_W),
                r3(rq), r3(rf), r3(ri), r3(rg), front(mrq), front(mrf), front(mri))

    wqt = peer_w_query[0].T.astype(BF16)
    sk = peer_subkeys[0].reshape(2 * PEER_HEADS, N_KEYS, -1).astype(BF16)
    h1, xt, st = _outproj(x2, att.reshape(n, GROUP_W), rec.reshape(n, GROUP_W),
                          w_out[0].astype(BF16), ffn_norm_w[0].reshape(1, D), wqt, sk, tm)

    lam, pw, r2, qw = _route(st, _pick(n, (256, 128)))

    tb = _pick(n, (512, 256, 128))
    te = PEER_EXPERT_TILE
    vt_tiles = peer_v[0].astype(BF16).reshape(N_EXPERTS // te, te, D).transpose(0, 2, 1)
    u_scaled = (peer_u[0] * (2.0 ** -0.5)).astype(BF16)
    out = _peer(xt, u_scaled, vt_tiles, lam, pw, r2, qw,
                h1, final_norm_w.reshape(1, D), tb, te)
    return out.reshape(B, S, D)
```

```python
import functools
import itertools
import math

import numpy as np
import jax
import jax.numpy as jnp
from jax import lax
from jax.experimental import pallas as pl
from jax.experimental.pallas import tpu as pltpu

F32 = jnp.float32
BF16 = jnp.bfloat16

D_MODEL = 1024
N_META = 16
ATT_HEADS = 4
ATT_HEAD_DIM = 64
REC_HEADS = 4
HEAD_W = 128
GROUP_W = 512
N_GROUPS = 7
ROPE_THETA = 10000.0
PEER_HEADS = 8
N_KEYS = 128
N_EXPERTS = N_KEYS * N_KEYS
PEER_TOPK = 16
EPS = 1e-6
LAM_INIT = 0.8 - 0.6 * math.exp(-0.3 * 0)
NEG = -1e30
REC_CHUNK = 128
ATT_HEADS_PER_STEP = 2
PEER_EXPERT_TILE = 2048
PEER_TOKEN_STRIP = 512
GELU_FOLD = 0.5 * 2.0 ** 0.5
PEER_HIDDEN_ROWS = 128
MXU_TILE = 256
VMEM_LIMIT = 56 * 1024 * 1024

_NT = (((1,), (1,)), ((), ()))


def _rms(x, w):
    return x * lax.rsqrt(jnp.mean(x * x, axis=-1, keepdims=True) + EPS) * w


def _inproj_kernel(x_ref, nw_ref, w_ref, cos_ref, sin_ref,
                   q_ref, k_ref, v_ref, rq_ref, rf_ref, ri_ref, rg_ref):
    a = _rms(x_ref[...], nw_ref[...]).astype(BF16)
    cos = cos_ref[...]
    sin = sin_ref[...]
    lane = lax.broadcasted_iota(jnp.int32, (1, HEAD_W), 1)
    lo = (lane % ATT_HEAD_DIM) < (ATT_HEAD_DIM // 2)

    def proj(c):
        return jnp.dot(a, w_ref[:, c * GROUP_W:(c + 1) * GROUP_W],
                       preferred_element_type=F32)

    def rope(y, scale):
        outs = []
        for h in range(ATT_HEADS):
            z = y[:, h * HEAD_W:(h + 1) * HEAD_W]
            rot = jnp.where(lo, pltpu.roll(z, HEAD_W - 32, 1), pltpu.roll(z, 32, 1))
            r = z * cos + rot * sin
            if scale is not None:
                r = r * scale
            outs.append(r)
        return jnp.concatenate(outs, axis=1)

    q_ref[...] = rope(proj(0), ATT_HEAD_DIM ** -0.5 * math.log2(math.e)).astype(q_ref.dtype)
    k_ref[...] = rope(proj(1), None).astype(k_ref.dtype)
    v_ref[...] = proj(2).astype(v_ref.dtype)
    rq_ref[...] = proj(3).astype(rq_ref.dtype)
    rf_ref[...] = proj(4).astype(rf_ref.dtype)
    ri_ref[...] = proj(5).astype(ri_ref.dtype)
    rg_ref[...] = proj(6).astype(rg_ref.dtype)


def _inproj(x2, nw, w_bf, cos_t, sin_t, tm):
    n = x2.shape[0]
    n_pos = cos_t.shape[0] // tm
    row = lambda i: (i, 0)
    fixed = lambda i: (0, 0)
    pos = lambda i: (i % n_pos, 0)
    dts = (BF16, BF16, BF16, BF16, F32, BF16, BF16)
    return pl.pallas_call(
        _inproj_kernel,
        grid=(n // tm,),
        in_specs=[pl.BlockSpec((tm, D_MODEL), row),
                  pl.BlockSpec((1, D_MODEL), fixed),
                  pl.BlockSpec((D_MODEL, N_GROUPS * GROUP_W), fixed),
                  pl.BlockSpec((tm, HEAD_W), pos),
                  pl.BlockSpec((tm, HEAD_W), pos)],
        out_specs=[pl.BlockSpec((tm, GROUP_W), row)] * N_GROUPS,
        out_shape=[jax.ShapeDtypeStruct((n, GROUP_W), dt) for dt in dts],
        compiler_params=pltpu.CompilerParams(
            dimension_semantics=("arbitrary",), vmem_limit_bytes=VMEM_LIMIT),
        name="inproj",
    )(x2, nw, w_bf, cos_t, sin_t)


def _attn_kernel(lam_ref, q_ref, k_ref, v_ref, km_ref, vm_ref, sw_ref, o_ref,
                 m_scr, acc_scr, s0_scr, s1_scr, *, bq, nh):
    qi = pl.program_id(2)
    heads = range(nh)
    sls = [slice(h * HEAD_W, (h + 1) * HEAD_W) for h in heads]
    lane = lax.broadcasted_iota(jnp.int32, (1, HEAD_W), 1)
    qq = []
    for h in heads:
        q = q_ref[:, sls[h]]
        zero = jnp.zeros_like(q)
        qq.append(jnp.concatenate([jnp.where(lane < ATT_HEAD_DIM, q, zero),
                                   jnp.where(lane >= ATT_HEAD_DIM, q, zero)], axis=0))

    m_scr[...] = jnp.full(m_scr.shape, NEG, F32)
    acc_scr[...] = jnp.zeros(acc_scr.shape, F32)

    def kv(ref, j, h):
        return ref[pl.ds(pl.multiple_of(j * bq, bq), bq), sls[h]]

    def scores(h, kb):
        return lax.dot_general(qq[h], kb, _NT, preferred_element_type=F32)

    def accumulate(h, s, vb, mask):
        if mask is not None:
            s = jnp.where(mask, s, NEG)
        m_prev = m_scr[h]
        m_new = jnp.maximum(m_prev, jnp.max(s, axis=1, keepdims=True))
        yield
        alpha = jnp.exp2(m_prev - m_new)
        p = jnp.exp2(s - jnp.concatenate([m_new] * (s.shape[1] // HEAD_W), axis=1))
        yield
        v_ext = jnp.concatenate([vb, jnp.ones_like(vb)], axis=1)
        acc_scr[h] = (jnp.concatenate([alpha, alpha], axis=1) * acc_scr[h]
                      + jnp.dot(p.astype(BF16), v_ext, preferred_element_type=F32))
        m_scr[h] = m_new

    def lockstep(gens):
        for _ in itertools.zip_longest(*gens):
            pass

    col_m = lax.broadcasted_iota(jnp.int32, (1, km_ref.shape[0]), 1)
    lockstep([accumulate(h, scores(h, km_ref[:, sls[h]]), vm_ref[:, sls[h]], col_m < N_META)
              for h in heads])

    for h in heads:
        s0_scr[h] = scores(h, kv(k_ref, 0, h))

    def body(jj, carry):
        j = 2 * jj
        for h in heads:
            s1_scr[h] = scores(h, kv(k_ref, j + 1, h))
        lockstep([accumulate(h, s0_scr[h], kv(v_ref, j, h), None) for h in heads])
        for h in heads:
            s0_scr[h] = scores(h, kv(k_ref, j + 2, h))
        lockstep([accumulate(h, s1_scr[h], kv(v_ref, j + 1, h), None) for h in heads])
        return carry

    lax.fori_loop(0, qi // 2, body, 0)

    row = lax.broadcasted_iota(jnp.int32, (2 * bq, bq), 0) % bq
    col = lax.broadcasted_iota(jnp.int32, (2 * bq, bq), 1)
    causal = col <= row

    @pl.when(qi % 2 == 0)
    def _():
        lockstep([accumulate(h, s0_scr[h], kv(v_ref, qi, h), causal) for h in heads])

    @pl.when(qi % 2 == 1)
    def _():
        for h in heads:
            s1_scr[h] = scores(h, kv(k_ref, qi, h))
        lockstep([accumulate(h, s0_scr[h], kv(v_ref, qi - 1, h), None) for h in heads])
        lockstep([accumulate(h, s1_scr[h], kv(v_ref, qi, h), causal) for h in heads])

    lv = lam_ref[...]
    lam = (jnp.exp(jnp.sum(lv[0:1] * lv[1:2], axis=1, keepdims=True))
           - jnp.exp(jnp.sum(lv[2:3] * lv[3:4], axis=1, keepdims=True)) + LAM_INIT)
    for h in heads:
        acc = acc_scr[h]
        num = acc[:, :HEAD_W]
        den = acc[:, HEAD_W:]
        o = num[:bq] / den[:bq] - lam * (num[bq:] / den[bq:])
        o_ref[:, sls[h]] = (_rms(o, sw_ref[...]) * (1.0 - LAM_INIT)).astype(o_ref.dtype)


def _attention(lam4, q, k, v, km, vm, sw, bq):
    B, S, _ = q.shape
    nh = ATT_HEADS_PER_STEP
    w = nh * HEAD_W
    kernel = functools.partial(_attn_kernel, bq=bq, nh=nh)
    return pl.pallas_call(
        kernel,
        grid=(B, ATT_HEADS // nh, S // bq),
        in_specs=[pl.BlockSpec((4, ATT_HEAD_DIM), lambda b, h, i: (0, 0)),
                  pl.BlockSpec((None, bq, w), lambda b, h, i: (b, i, h)),
                  pl.BlockSpec((None, S, w), lambda b, h, i: (b, 0, h)),
                  pl.BlockSpec((None, S, w), lambda b, h, i: (b, 0, h)),
                  pl.BlockSpec((km.shape[0], w), lambda b, h, i: (0, h)),
                  pl.BlockSpec((km.shape[0], w), lambda b, h, i: (0, h)),
                  pl.BlockSpec((1, HEAD_W), lambda b, h, i: (0, 0))],
        out_specs=pl.BlockSpec((None, bq, w), lambda b, h, i: (b, i, h)),
        out_shape=jax.ShapeDtypeStruct((B, S, GROUP_W), BF16),
        scratch_shapes=[pltpu.VMEM((nh, 2 * bq, HEAD_W), F32),
                        pltpu.VMEM((nh, 2 * bq, 2 * HEAD_W), F32),
                        pltpu.VMEM((nh, 2 * bq, bq), F32),
                        pltpu.VMEM((nh, 2 * bq, bq), F32)],
        compiler_params=pltpu.CompilerParams(
            dimension_semantics=("arbitrary", "arbitrary", "arbitrary"),
            vmem_limit_bytes=VMEM_LIMIT),
        name="diff_attention",
    )(lam4, q, k, v, km, vm, sw)


def _cum_matrix():
    r = np.arange(REC_CHUNK)
    return (r[None, :] <= r[:, None]).astype(np.float32)


def _block_rows(cum, period, offset):
    return jnp.concatenate(
        [jnp.broadcast_to(cum[b * period + offset:b * period + offset + 1], (period, cum.shape[1]))
         for b in range(cum.shape[0] // period)], axis=0)


def _gates(rq, rf, lb):
    q = rq * jax.nn.sigmoid(rq)
    f = lb + (1.0 - lb) * jax.nn.sigmoid(rf)
    return q, 1.0 - f, jnp.log(f)


def _hgrn_kernel(cm_ref, lb_ref, nw_ref, rq_ref, rf_ref, ri_ref, rg_ref,
                 mq_ref, mf_ref, mi_ref, o_ref, st_scr):
    c_idx = pl.program_id(1)
    C = REC_CHUNK
    tri = cm_ref[...]

    def cumsum(g):
        g0 = g.astype(BF16)
        r1 = g - g0.astype(F32)
        g1 = r1.astype(BF16)
        g2 = (r1 - g1.astype(F32)).astype(BF16)
        return (jnp.dot(tri, g0, preferred_element_type=F32)
                + jnp.dot(tri, g1, preferred_element_type=F32)
                + jnp.dot(tri, g2, preferred_element_type=F32))

    def state_update(st, k, v32, cum):
        last = cum[C - 1:C]
        ke = (k * jnp.exp(last - cum)).astype(BF16)
        upd = jnp.dot(v32.T.astype(BF16), ke, preferred_element_type=F32)
        return st * jnp.exp(last) + upd

    @pl.when(c_idx == 0)
    def _():
        valid = lax.broadcasted_iota(jnp.int32, (C, 1), 0) >= C - N_META
        for h in range(REC_HEADS):
            sl = slice(h * HEAD_W, (h + 1) * HEAD_W)
            _, k, g = _gates(mq_ref[:, sl].astype(F32), mf_ref[:, sl], lb_ref[:, sl])
            k = jnp.where(valid, k, 0.0)
            g = jnp.where(valid, g, 0.0)
            v32 = jnp.where(valid, mi_ref[:, sl].astype(F32), 0.0)
            cum = cumsum(g)
            st_scr[h] = state_update(jnp.zeros((HEAD_W, HEAD_W), F32), k, v32, cum)

    t = lax.broadcasted_iota(jnp.int32, (C, C), 0)
    s = lax.broadcasted_iota(jnp.int32, (C, C), 1)
    mask_d = (t // 16 == s // 16) & (s <= t)
    masks = [(t // (2 * hf) == s // (2 * hf)) & ((t // hf) % 2 == 1) & ((s // hf) % 2 == 0)
             for hf in (16, 32, 64)]

    def head_stages(h):
        sl = slice(h * HEAD_W, (h + 1) * HEAD_W)
        q, k, g = _gates(rq_ref[:, sl].astype(F32), rf_ref[:, sl], lb_ref[:, sl])
        v32 = ri_ref[:, sl].astype(F32)
        cum = cumsum(g)
        yield
        st = st_scr[h]
        qe = (q * jnp.exp(cum)).astype(BF16)
        o = lax.dot_general(qe, st.astype(BF16), _NT, preferred_element_type=F32)
        yield

        def pair_scores(ref, clamp):
            dq = cum - ref
            dk = ref - cum
            if clamp:
                dq = jnp.minimum(dq, 0.0)
                dk = jnp.minimum(dk, 0.0)
            return lax.dot_general((q * jnp.exp(dq)).astype(BF16),
                                   (k * jnp.exp(dk)).astype(BF16), _NT,
                                   preferred_element_type=F32)

        a = jnp.where(mask_d, pair_scores(_block_rows(cum, 16, 7), False), 0.0)
        yield
        for hf, mk in zip((16, 32, 64), masks):
            a = jnp.where(mk, pair_scores(_block_rows(cum, 2 * hf, hf - 1), True), a)
            yield
        o = o + jnp.dot(a.astype(BF16), v32.astype(BF16), preferred_element_type=F32)
        st_scr[h] = state_update(st, k, v32, cum)
        yield
        rg = rg_ref[:, sl].astype(F32)
        o = _rms(o, nw_ref[:, sl]) * (rg * jax.nn.sigmoid(rg))
        o_ref[:, sl] = o.astype(o_ref.dtype)

    for _ in itertools.zip_longest(*[head_stages(h) for h in range(REC_HEADS)]):
        pass


def _hgrn(cm, lb, nw, rq, rf, ri, rg, mq, mf, mi):
    B, S, _ = rq.shape
    C = REC_CHUNK
    tok = pl.BlockSpec((None, C, GROUP_W), lambda b, c: (b, c, 0))
    fix = lambda shape: pl.BlockSpec(shape, lambda b, c: (0, 0))
    return pl.pallas_call(
        _hgrn_kernel,
        grid=(B, S // C),
        in_specs=[fix(cm.shape), fix((1, GROUP_W)), fix((1, GROUP_W)),
                  tok, tok, tok, tok,
                  fix((C, GROUP_W)), fix((C, GROUP_W)), fix((C, GROUP_W))],
        out_specs=tok,
        out_shape=jax.ShapeDtypeStruct((B, S, GROUP_W), BF16),
        scratch_shapes=[pltpu.VMEM((REC_HEADS, HEAD_W, HEAD_W), F32)],
        compiler_params=pltpu.CompilerParams(
            dimension_semantics=("arbitrary", "arbitrary"),
            vmem_limit_bytes=VMEM_LIMIT),
        name="hgrn2",
    )(cm, lb, nw, rq, rf, ri, rg, mq, mf, mi)


def _outproj_kernel(x_ref, att_ref, rec_ref, wo_ref, fw_ref, wq_ref, sk_ref,
                    h_ref, xt_ref, st_ref):
    h1 = (x_ref[...]
          + jnp.dot(att_ref[...], wo_ref[:GROUP_W, :], preferred_element_type=F32)
          + jnp.dot(rec_ref[...], wo_ref[GROUP_W:, :], preferred_element_type=F32))
    h_ref[...] = h1
    xt = _rms(h1, fw_ref[...]).T.astype(BF16)
    xt_ref[...] = xt
    qt = jnp.dot(wq_ref[...], xt, preferred_element_type=F32).astype(BF16)
    for g in range(2 * PEER_HEADS):
        sl = slice(g * N_KEYS, (g + 1) * N_KEYS)
        st_ref[sl, :] = jnp.dot(sk_ref[g], qt[sl, :], preferred_element_type=F32)


def _outproj(x2, att, rec, wo, fw, wqt, sk, tm):
    n = x2.shape[0]
    row = lambda i: (i, 0)
    colb = lambda i: (0, i)
    fixed = lambda i: (0, 0)
    return pl.pallas_call(
        _outproj_kernel,
        grid=(n // tm,),
        in_specs=[pl.BlockSpec((tm, D_MODEL), row),
                  pl.BlockSpec((tm, GROUP_W), row),
                  pl.BlockSpec((tm, GROUP_W), row),
                  pl.BlockSpec((D_MODEL, D_MODEL), fixed),
                  pl.BlockSpec((1, D_MODEL), fixed),
                  pl.BlockSpec(wqt.shape, fixed),
                  pl.BlockSpec(sk.shape, lambda i: (0, 0, 0))],
        out_specs=[pl.BlockSpec((tm, D_MODEL), row),
                   pl.BlockSpec((D_MODEL, tm), colb),
                   pl.BlockSpec((wqt.shape[0], tm), colb)],
        out_shape=[jax.ShapeDtypeStruct((n, D_MODEL), F32),
                   jax.ShapeDtypeStruct((D_MODEL, n), BF16),
                   jax.ShapeDtypeStruct((wqt.shape[0], n), F32)],
        compiler_params=pltpu.CompilerParams(
            dimension_semantics=("arbitrary",), vmem_limit_bytes=VMEM_LIMIT),
        name="outproj_scores",
    )(x2, att, rec, wo, fw, wqt, sk)


def _staircase():
    return [(p, q) for p in range(PEER_TOPK) for q in range(PEER_TOPK)
            if (p + 1) * (q + 1) <= PEER_TOPK]


def _sort_network(n):
    pairs = []
    p = 1
    while p < n:
        k = p
        while k >= 1:
            for j in range(k % p, n - k, 2 * k):
                for i in range(min(k, n - j - k)):
                    if (i + j) // (2 * p) == (i + j + k) // (2 * p):
                        pairs.append((i + j, i + j + k))
            k //= 2
        p *= 2
    return pairs


def _sorted_desc(xs):
    xs = list(xs)
    for i, j in _sort_network(len(xs)):
        xs[i], xs[j] = jnp.maximum(xs[i], xs[j]), jnp.minimum(xs[i], xs[j])
    return xs


def _merge_top(a, b):
    n = len(a)
    xs = [jnp.maximum(a[i], b[n - 1 - i]) for i in range(n)]
    d = n // 2
    while d >= 1:
        for i in range(n):
            if (i // d) % 2 == 0:
                xs[i], xs[i + d] = jnp.maximum(xs[i], xs[i + d]), jnp.minimum(xs[i], xs[i + d])
        d //= 2
    return xs


def _prefix_count(tops, test):
    assert len(tops) == 16

    def pick(bits, lo):
        step = 8
        idx = [lo]
        for _ in bits:
            idx = [i + d for i in idx for d in (0, step)]
            step //= 2
        vals = [tops[i] for i in idx]
        for b in reversed(bits):
            vals = [jnp.where(b, vals[2 * k + 1], vals[2 * k]) for k in range(len(vals) // 2)]
        return vals[0]

    t1 = test(tops[7])
    t2 = test(pick([t1], 3))
    t3 = test(pick([t1, t2], 1))
    t4 = test(pick([t1, t2, t3], 0))
    t5 = test(tops[15])
    one = lambda t, w: jnp.where(t, float(w), 0.0)
    return one(t1, 8) + one(t2, 4) + one(t3, 2) + one(t4, 1) + one(t5, 1)


def _route_kernel(st_ref, lam_ref, pw_ref, r2_ref, qw_ref, thr_scr, top_scr):
    K = PEER_TOPK
    SUB = 8
    tb = st_ref.shape[1]
    pairs = _staircase()

    def key_slab(g, v):
        return st_ref[g * N_KEYS + v * SUB:g * N_KEYS + (v + 1) * SUB, :]

    for g in range(2 * PEER_HEADS):
        srt = _sorted_desc([key_slab(g, v) for v in range(N_KEYS // SUB)])
        for shift in (4, 2, 1):
            srt = _merge_top(srt, [pltpu.roll(x, shift, 0) for x in srt])
        for p in range(K):
            top_scr[g, p] = srt[p]

    class _Lazy:
        def __init__(self, g):
            self.g = g

        def __getitem__(self, p):
            return top_scr[self.g, p]

        def __len__(self):
            return K

    tops = [_Lazy(g) for g in range(2 * PEER_HEADS)]

    sub = lax.broadcasted_iota(jnp.int32, (PEER_HEADS, tb), 0)

    def stack_heads(part, p):
        out = tops[part][p]
        for h in range(1, PEER_HEADS):
            out = jnp.where(sub == h, tops[2 * h + part][p], out)
        return out

    a = [stack_heads(0, p) for p in range(K)]
    b = [stack_heads(1, p) for p in range(K)]
    sums = [a[p] + b[q] for (p, q) in pairs]
    n_full = len(sums) // K * K
    top = None
    for k in range(0, n_full, K):
        run = _sorted_desc(sums[k:k + K])
        top = run if top is None else _merge_top(top, run)
    for extra in sums[n_full:]:
        top = [jnp.maximum(top[0], extra)] + [
            jnp.maximum(top[i], jnp.minimum(top[i - 1], extra)) for i in range(1, K)]
    thr_scr[...] = top[K - 1]
    thr = thr_scr[...]
    m0 = sums[0]
    z = jnp.zeros((PEER_HEADS, tb), F32)
    for v in sums:
        z = z + jnp.where(v >= thr, jnp.exp(v - m0), 0.0)
    inv_z = 1.0 / z

    for h in range(PEER_HEADS):
        top1, top2 = tops[2 * h], tops[2 * h + 1]
        thr_h = jnp.broadcast_to(thr[h:h + 1, :], (SUB, tb))
        inv_z_h = jnp.broadcast_to(inv_z[h:h + 1, :], (SUB, tb))
        for v in range(N_KEYS // SUB):
            rows = slice(v * SUB, (v + 1) * SUB)
            s1 = key_slab(2 * h, v)
            lam_ref[h, rows, :] = _prefix_count(top2, lambda bq: s1 + bq >= thr_h)
            pw_ref[h, rows, :] = jnp.exp(s1 - top1[0]) * inv_z_h * GELU_FOLD
        for v in range(0, N_KEYS // SUB, 2):
            rows = slice(v * SUB, (v + 2) * SUB)
            s2 = jnp.concatenate([key_slab(2 * h + 1, v), key_slab(2 * h + 1, v + 1)], axis=0)
            top2_w = [jnp.concatenate([top2[p], top2[p]], axis=0) for p in range(K)]
            rank = _prefix_count(top2_w, lambda bp: s2 < bp)
            r2_ref[h, rows, :] = rank.astype(r2_ref.dtype)
            qw_ref[h, rows, :] = jnp.exp(s2 - top2_w[0]).astype(qw_ref.dtype)


def _route(st, tb):
    n = st.shape[1]
    spec = pl.BlockSpec((PEER_HEADS, N_KEYS, tb), lambda i: (0, 0, i))
    shp = lambda dt: jax.ShapeDtypeStruct((PEER_HEADS, N_KEYS, n), dt)
    return pl.pallas_call(
        _route_kernel,
        grid=(n // tb,),
        in_specs=[pl.BlockSpec((st.shape[0], tb), lambda i: (0, i))],
        out_specs=[spec, spec, spec, spec],
        out_shape=[shp(F32), shp(F32), shp(BF16), shp(BF16)],
        scratch_shapes=[pltpu.VMEM((PEER_HEADS, tb), F32),
                        pltpu.VMEM((2 * PEER_HEADS, PEER_TOPK, 8, tb), F32)],
        compiler_params=pltpu.CompilerParams(
            dimension_semantics=("arbitrary",), vmem_limit_bytes=VMEM_LIMIT),
        name="peer_route",
    )(st)


def _peer_kernel(xt_ref, u_ref, vt_ref, lam_ref, pw_ref, r2_ref, qw_ref,
                 h_ref, fw_ref, o_ref, acc_scr, hid0_scr, hid1_scr, *, te, n_e):
    e = pl.program_id(1)
    tb = xt_ref.shape[1]
    n_grp = te // N_KEYS
    hid_bufs = (hid0_scr, hid1_scr)

    def row_bcast(row):
        tile = jnp.broadcast_to(row, (16, row.shape[1])).astype(BF16)
        return jnp.concatenate([tile] * (N_KEYS // 16), axis=0)

    def hidden_pieces(par):
        hid_scr = hid_bufs[par]
        pieces = []
        tw = min(tb, PEER_TOKEN_STRIP)
        gpp = PEER_HIDDEN_ROWS // N_KEYS
        for i0, ct in itertools.product(range(0, n_grp, gpp), range(tb // tw)):
            def piece(i0=i0, ct=ct):
                cols = slice(ct * tw, (ct + 1) * tw)
                prow = slice(i0 * N_KEYS, (i0 + gpp) * N_KEYS)
                ht_all = jnp.dot(u_ref[prow, :], xt_ref[:, cols], preferred_element_type=F32)
                for g in range(gpp):
                    rows = slice((i0 + g) * N_KEYS, (i0 + g + 1) * N_KEYS)
                    i = e * n_grp + i0 + g
                    ht = ht_all[g * N_KEYS:(g + 1) * N_KEYS]
                    act = (ht * (1.0 + lax.erf(ht))).astype(BF16)
                    gate = None
                    for h in range(PEER_HEADS):
                        lam_b = row_bcast(lam_ref[h, pl.ds(i, 1), :][:, cols])
                        p_b = row_bcast(pw_ref[h, pl.ds(i, 1), :][:, cols])
                        sel = jnp.where(r2_ref[h, :, cols] < lam_b, qw_ref[h, :, cols],
                                        jnp.zeros_like(p_b))
                        gate = sel * p_b if gate is None else gate + sel * p_b
                    hid_scr[rows, cols] = act * gate
            pieces.append(piece)
        return pieces

    def value_pieces(par):
        hid_scr = hid_bufs[par]
        k_tiles = te // MXU_TILE
        pieces = []
        for nt in range(tb // MXU_TILE):
            cols = slice(nt * MXU_TILE, (nt + 1) * MXU_TILE)
            state = {}
            for kt in range(k_tiles):
                ks = slice(kt * MXU_TILE, (kt + 1) * MXU_TILE)

                def piece(ks=ks, cols=cols, kt=kt, state=state):
                    part = jnp.dot(vt_ref[:, ks], hid_scr[ks, cols], preferred_element_type=F32)
                    state["sum"] = part if kt == 0 else state["sum"] + part
                    if kt == k_tiles - 1:
                        acc_scr[:, cols] += state.pop("sum")
                pieces.append(piece)
        return pieces

    def run(*stages):
        order = sorted(((k + 0.5) / len(st), s_idx, k)
                       for s_idx, st in enumerate(stages) for k in range(len(st)))
        for _, s_idx, k in order:
            stages[s_idx][k]()

    @pl.when(e == 0)
    def _():
        acc_scr[...] = jnp.zeros(acc_scr.shape, F32)
        run(hidden_pieces(0))

    for par in (0, 1):
        @pl.when((e >= 1) & (e < n_e) & (e % 2 == par))
        def _(par=par):
            run(value_pieces(1 - par), hidden_pieces(par))

    @pl.when(e == n_e)
    def _():
        run(value_pieces(1 - n_e % 2))
        o_ref[...] = _rms(h_ref[...] + acc_scr[...].T, fw_ref[...])


def _peer(xt, u_bf, vt_bf, lam, pw, r2, qw, h1, fw, tb, te):
    n = xt.shape[1]
    n_e = N_EXPERTS // te
    assert n_e >= 2
    kernel = functools.partial(_peer_kernel, te=te, n_e=n_e)
    rt = pl.BlockSpec((PEER_HEADS, N_KEYS, tb), lambda t, e: (0, 0, t))
    return pl.pallas_call(
        kernel,
        grid=(n // tb, n_e + 1),
        in_specs=[pl.BlockSpec((D_MODEL, tb), lambda t, e: (0, t)),
                  pl.BlockSpec((te, D_MODEL), lambda t, e: (jnp.minimum(e, n_e - 1), 0)),
                  pl.BlockSpec((None, D_MODEL, te), lambda t, e: (jnp.maximum(e - 1, 0), 0, 0)),
                  rt, rt, rt, rt,
                  pl.BlockSpec((tb, D_MODEL), lambda t, e: (t, 0)),
                  pl.BlockSpec((1, D_MODEL), lambda t, e: (0, 0))],
        out_specs=pl.BlockSpec((tb, D_MODEL), lambda t, e: (t, 0)),
        out_shape=jax.ShapeDtypeStruct((n, D_MODEL), F32),
        scratch_shapes=[pltpu.VMEM((D_MODEL, tb), F32),
                        pltpu.VMEM((te, tb), BF16), pltpu.VMEM((te, tb), BF16)],
        compiler_params=pltpu.CompilerParams(
            dimension_semantics=("arbitrary", "arbitrary"),
            vmem_limit_bytes=VMEM_LIMIT),
        name="peer_experts",
    )(xt, u_bf, vt_bf, lam, pw, r2, qw, h1, fw)


def _rope_tables(T):
    d = ATT_HEAD_DIM
    inv_freq = ROPE_THETA ** (-jnp.arange(0, d, 2, dtype=F32) / d)
    ang = jnp.arange(T, dtype=F32)[:, None] * inv_freq[None, :]
    ang = jnp.concatenate([ang, ang], axis=-1)
    sign = jnp.where(jnp.arange(d) < d // 2, -1.0, 1.0).astype(F32)
    cos = jnp.tile(jnp.cos(ang), (1, 2))
    sin = jnp.tile(jnp.sin(ang) * sign[None, :], (1, 2))
    return cos, sin


def _pick(n, prefs):
    for p in prefs:
        if n % p == 0:
            return p
    raise ValueError(f"no supported tile for extent {n}")


def kernel(x, meta_tokens, mix_norm_w, w_in, rec_lb_logits, rec_norm_w, diff_lambda_q1, diff_lambda_k1, diff_lambda_q2, diff_lambda_k2, diff_subln_w, w_out, ffn_norm_w, peer_w_query, peer_subkeys, peer_u, peer_v, final_norm_w):
    B, S, D = x.shape
    assert D == D_MODEL and w_in.shape[0] == 1 and S % REC_CHUNK == 0
    n = B * S
    x2 = x.reshape(n, D)

    cos, sin = _rope_tables(N_META + S)
    w_bf = w_in[0].astype(BF16)
    nw = mix_norm_w[0].reshape(1, D)

    tm = _pick(S, (512, 256, 128))
    q, k, v, rq, rf, ri, rg = _inproj(x2, nw, w_bf, cos[N_META:], sin[N_META:], tm)
    mq_a, mk_a, mv_a, mrq, mrf, mri, _ = _inproj(
        meta_tokens.astype(F32), nw, w_bf, cos[:N_META], sin[:N_META], N_META)

    bq = _pick(S, (512, 256, 128))
    pad_m = lambda z: jnp.pad(z, ((0, HEAD_W - N_META), (0, 0)))
    lam4 = jnp.stack([diff_lambda_q1[0], diff_lambda_k1[0],
                      diff_lambda_q2[0], diff_lambda_k2[0]]).astype(F32)
    sw = diff_subln_w[0].reshape(1, HEAD_W)
    r3 = lambda z: z.reshape(B, S, GROUP_W)
    att = _attention(lam4, r3(q), r3(k), r3(v), pad_m(mk_a), pad_m(mv_a), sw, bq)

    lb = jax.nn.softmax(rec_lb_logits.astype(F32), axis=0)[0].reshape(1, GROUP_W)
    front = lambda z: jnp.pad(z, ((REC_CHUNK - N_META, 0), (0, 0)))
    rec = _hgrn(jnp.asarray(_cum_matrix(), BF16), lb, rec_norm_w[0].reshape(1, GROUP_W),
                r3(rq), r3(rf), r3(ri), r3(rg), front(mrq), front(mrf), front(mri))

    wqt = peer_w_query[0].T.astype(BF16)
    sk = peer_subkeys[0].reshape(2 * PEER_HEADS, N_KEYS, -1).astype(BF16)
    h1, xt, st = _outproj(x2, att.reshape(n, GROUP_W), rec.reshape(n, GROUP_W),
                          w_out[0].astype(BF16), ffn_norm_w[0].reshape(1, D), wqt, sk, tm)

    lam, pw, r2, qw = _route(st, _pick(n, (256, 128)))

    tb = _pick(n, (512, 256, 128))
    te = PEER_EXPERT_TILE
    vt_tiles = peer_v[0].astype(BF16).reshape(N_EXPERTS // te, te, D).transpose(0, 2, 1)
    u_scaled = (peer_u[0] * (2.0 ** -0.5)).astype(BF16)
    out = _peer(xt, u_scaled, vt_tiles, lam, pw, r2, qw,
                h1, final_norm_w.reshape(1, D), tb, te)
    return out.reshape(B, S, D)
```

```python
import functools
import itertools
import math

import numpy as np
import jax
import jax.numpy as jnp
from jax import lax
from jax.experimental import pallas as pl
from jax.experimental.pallas import tpu as pltpu

F32 = jnp.float32
BF16 = jnp.bfloat16

D_MODEL = 1024
N_META = 16
ATT_HEADS = 4
ATT_HEAD_DIM = 64
REC_HEADS = 4
HEAD_W = 128
GROUP_W = 512
N_GROUPS = 7
ROPE_THETA = 10000.0
PEER_HEADS = 8
N_KEYS = 128
N_EXPERTS = N_KEYS * N_KEYS
PEER_TOPK = 16
EPS = 1e-6
LAM_INIT = 0.8 - 0.6 * math.exp(-0.3 * 0)
NEG = -1e30
REC_CHUNK = 128
ATT_HEADS_PER_STEP = 2
PEER_EXPERT_TILE = 2048
PEER_TOKEN_STRIP = 512
GELU_FOLD = 0.5 * 2.0 ** 0.5
PEER_HIDDEN_ROWS = 128
MXU_TILE = 256
F32_SUBLANES = 8
BF16_TILE_ROWS = 16
VMEM_LIMIT = 56 * 1024 * 1024

_NT = (((1,), (1,)), ((), ()))


def _rms(x, w):
    return x * lax.rsqrt(jnp.mean(x * x, axis=-1, keepdims=True) + EPS) * w


def _inproj_kernel(x_ref, nw_ref, w_ref, cos_ref, sin_ref,
                   q_ref, k_ref, v_ref, rq_ref, rf_ref, ri_ref, rg_ref):
    a = _rms(x_ref[...], nw_ref[...]).astype(BF16)
    cos = cos_ref[...]
    sin = sin_ref[...]
    lane = lax.broadcasted_iota(jnp.int32, (1, HEAD_W), 1)
    lo = (lane % ATT_HEAD_DIM) < (ATT_HEAD_DIM // 2)

    def proj(c):
        return jnp.dot(a, w_ref[:, c * GROUP_W:(c + 1) * GROUP_W],
                       preferred_element_type=F32)

    def rope(y, scale):
        outs = []
        for h in range(ATT_HEADS):
            z = y[:, h * HEAD_W:(h + 1) * HEAD_W]
            rot = jnp.where(lo, pltpu.roll(z, HEAD_W - 32, 1), pltpu.roll(z, 32, 1))
            r = z * cos + rot * sin
            if scale is not None:
                r = r * scale
            outs.append(r)
        return jnp.concatenate(outs, axis=1)

    q_ref[...] = rope(proj(0), ATT_HEAD_DIM ** -0.5 * math.log2(math.e)).astype(q_ref.dtype)
    k_ref[...] = rope(proj(1), None).astype(k_ref.dtype)
    v_ref[...] = proj(2).astype(v_ref.dtype)
    rq_ref[...] = proj(3).astype(rq_ref.dtype)
    rf_ref[...] = proj(4).astype(rf_ref.dtype)
    ri_ref[...] = proj(5).astype(ri_ref.dtype)
    rg_ref[...] = proj(6).astype(rg_ref.dtype)


def _inproj(x2, nw, w_bf, cos_t, sin_t, tm):
    n = x2.shape[0]
    n_pos = cos_t.shape[0] // tm
    row = lambda i: (i, 0)
    fixed = lambda i: (0, 0)
    pos = lambda i: (i % n_pos, 0)
    dts = (BF16, BF16, BF16, BF16, F32, BF16, BF16)
    return pl.pallas_call(
        _inproj_kernel,
        grid=(n // tm,),
        in_specs=[pl.BlockSpec((tm, D_MODEL), row),
                  pl.BlockSpec((1, D_MODEL), fixed),
                  pl.BlockSpec((D_MODEL, N_GROUPS * GROUP_W), fixed),
                  pl.BlockSpec((tm, HEAD_W), pos),
                  pl.BlockSpec((tm, HEAD_W), pos)],
        out_specs=[pl.BlockSpec((tm, GROUP_W), row)] * N_GROUPS,
        out_shape=[jax.ShapeDtypeStruct((n, GROUP_W), dt) for dt in dts],
        compiler_params=pltpu.CompilerParams(
            dimension_semantics=("arbitrary",), vmem_limit_bytes=VMEM_LIMIT),
        name="inproj",
    )(x2, nw, w_bf, cos_t, sin_t)


def _attn_kernel(lam_ref, q_ref, k_ref, v_ref, km_ref, vm_ref, sw_ref, o_ref,
                 m_scr, acc_scr, s0_scr, s1_scr, *, bq, nh):
    qi = pl.program_id(2)
    heads = range(nh)
    sls = [slice(h * HEAD_W, (h + 1) * HEAD_W) for h in heads]
    lane = lax.broadcasted_iota(jnp.int32, (1, HEAD_W), 1)
    qq = []
    for h in heads:
        q = q_ref[:, sls[h]]
        zero = jnp.zeros_like(q)
        qq.append(jnp.concatenate([jnp.where(lane < ATT_HEAD_DIM, q, zero),
                                   jnp.where(lane >= ATT_HEAD_DIM, q, zero)], axis=0))

    m_scr[...] = jnp.full(m_scr.shape, NEG, F32)
    acc_scr[...] = jnp.zeros(acc_scr.shape, F32)

    def kv(ref, j, h):
        return ref[pl.ds(pl.multiple_of(j * bq, bq), bq), sls[h]]

    def scores(h, kb):
        return lax.dot_general(qq[h], kb, _NT, preferred_element_type=F32)

    def accumulate(h, s, vb, mask):
        if mask is not None:
            s = jnp.where(mask, s, NEG)
        m_prev = m_scr[h]
        m_new = jnp.maximum(m_prev, jnp.max(s, axis=1, keepdims=True))
        yield
        alpha = jnp.exp2(m_prev - m_new)
        p = jnp.exp2(s - jnp.concatenate([m_new] * (s.shape[1] // HEAD_W), axis=1))
        yield
        v_ext = jnp.concatenate([vb, jnp.ones_like(vb)], axis=1)
        acc_scr[h] = (jnp.concatenate([alpha, alpha], axis=1) * acc_scr[h]
                      + jnp.dot(p.astype(BF16), v_ext, preferred_element_type=F32))
        m_scr[h] = m_new

    def lockstep(gens):
        for _ in itertools.zip_longest(*gens):
            pass

    col_m = lax.broadcasted_iota(jnp.int32, (1, km_ref.shape[0]), 1)
    lockstep([accumulate(h, scores(h, km_ref[:, sls[h]]), vm_ref[:, sls[h]], col_m < N_META)
              for h in heads])

    for h in heads:
        s0_scr[h] = scores(h, kv(k_ref, 0, h))

    def body(jj, carry):
        j = 2 * jj
        for h in heads:
            s1_scr[h] = scores(h, kv(k_ref, j + 1, h))
        lockstep([accumulate(h, s0_scr[h], kv(v_ref, j, h), None) for h in heads])
        for h in heads:
            s0_scr[h] = scores(h, kv(k_ref, j + 2, h))
        lockstep([accumulate(h, s1_scr[h], kv(v_ref, j + 1, h), None) for h in heads])
        return carry

    lax.fori_loop(0, qi // 2, body, 0)

    row = lax.broadcasted_iota(jnp.int32, (2 * bq, bq), 0) % bq
    col = lax.broadcasted_iota(jnp.int32, (2 * bq, bq), 1)
    causal = col <= row

    @pl.when(qi % 2 == 0)
    def _():
        lockstep([accumulate(h, s0_scr[h], kv(v_ref, qi, h), causal) for h in heads])

    @pl.when(qi % 2 == 1)
    def _():
        for h in heads:
            s1_scr[h] = scores(h, kv(k_ref, qi, h))
        lockstep([accumulate(h, s0_scr[h], kv(v_ref, qi - 1, h), None) for h in heads])
        lockstep([accumulate(h, s1_scr[h], kv(v_ref, qi, h), causal) for h in heads])

    lv = lam_ref[...]
    lam = (jnp.exp(jnp.sum(lv[0:1] * lv[1:2], axis=1, keepdims=True))
           - jnp.exp(jnp.sum(lv[2:3] * lv[3:4], axis=1, keepdims=True)) + LAM_INIT)
    for h in heads:
        acc = acc_scr[h]
        num = acc[:, :HEAD_W]
        den = acc[:, HEAD_W:]
        o = num[:bq] / den[:bq] - lam * (num[bq:] / den[bq:])
        o_ref[:, sls[h]] = (_rms(o, sw_ref[...]) * (1.0 - LAM_INIT)).astype(o_ref.dtype)


def _attention(lam4, q, k, v, km, vm, sw, bq):
    B, S, _ = q.shape
    nh = ATT_HEADS_PER_STEP
    w = nh * HEAD_W
    kernel = functools.partial(_attn_kernel, bq=bq, nh=nh)
    return pl.pallas_call(
        kernel,
        grid=(B, ATT_HEADS // nh, S // bq),
        in_specs=[pl.BlockSpec((4, ATT_HEAD_DIM), lambda b, h, i: (0, 0)),
                  pl.BlockSpec((None, bq, w), lambda b, h, i: (b, i, h)),
                  pl.BlockSpec((None, S, w), lambda b, h, i: (b, 0, h)),
                  pl.BlockSpec((None, S, w), lambda b, h, i: (b, 0, h)),
                  pl.BlockSpec((km.shape[0], w), lambda b, h, i: (0, h)),
                  pl.BlockSpec((km.shape[0], w), lambda b, h, i: (0, h)),
                  pl.BlockSpec((1, HEAD_W), lambda b, h, i: (0, 0))],
        out_specs=pl.BlockSpec((None, bq, w), lambda b, h, i: (b, i, h)),
        out_shape=jax.ShapeDtypeStruct((B, S, GROUP_W), BF16),
        scratch_shapes=[pltpu.VMEM((nh, 2 * bq, HEAD_W), F32),
                        pltpu.VMEM((nh, 2 * bq, 2 * HEAD_W), F32),
                        pltpu.VMEM((nh, 2 * bq, bq), F32),
                        pltpu.VMEM((nh, 2 * bq, bq), F32)],
        compiler_params=pltpu.CompilerParams(
            dimension_semantics=("arbitrary", "arbitrary", "arbitrary"),
            vmem_limit_bytes=VMEM_LIMIT),
        name="diff_attention",
    )(lam4, q, k, v, km, vm, sw)


def _cum_matrix():
    r = np.arange(REC_CHUNK)
    return (r[None, :] <= r[:, None]).astype(np.float32)


def _block_rows(cum, period, offset):
    return jnp.concatenate(
        [jnp.broadcast_to(cum[b * period + offset:b * period + offset + 1], (period, cum.shape[1]))
         for b in range(cum.shape[0] // period)], axis=0)


def _gates(rq, rf, lb):
    q = rq * jax.nn.sigmoid(rq)
    f = lb + (1.0 - lb) * jax.nn.sigmoid(rf)
    return q, 1.0 - f, jnp.log(f)


def _hgrn_kernel(cm_ref, lb_ref, nw_ref, rq_ref, rf_ref, ri_ref, rg_ref,
                 mq_ref, mf_ref, mi_ref, o_ref, st_scr):
    c_idx = pl.program_id(1)
    C = REC_CHUNK
    tri = cm_ref[...]

    def cumsum(g):
        g0 = g.astype(BF16)
        r1 = g - g0.astype(F32)
        g1 = r1.astype(BF16)
        g2 = (r1 - g1.astype(F32)).astype(BF16)
        return (jnp.dot(tri, g0, preferred_element_type=F32)
                + jnp.dot(tri, g1, preferred_element_type=F32)
                + jnp.dot(tri, g2, preferred_element_type=F32))

    def state_update(st, k, v32, cum):
        last = cum[C - 1:C]
        ke = (k * jnp.exp(last - cum)).astype(BF16)
        upd = jnp.dot(v32.T.astype(BF16), ke, preferred_element_type=F32)
        return st * jnp.exp(last) + upd

    @pl.when(c_idx == 0)
    def _():
        valid = lax.broadcasted_iota(jnp.int32, (C, 1), 0) >= C - N_META
        for h in range(REC_HEADS):
            sl = slice(h * HEAD_W, (h + 1) * HEAD_W)
            _, k, g = _gates(mq_ref[:, sl].astype(F32), mf_ref[:, sl], lb_ref[:, sl])
            k = jnp.where(valid, k, 0.0)
            g = jnp.where(valid, g, 0.0)
            v32 = jnp.where(valid, mi_ref[:, sl].astype(F32), 0.0)
            cum = cumsum(g)
            st_scr[h] = state_update(jnp.zeros((HEAD_W, HEAD_W), F32), k, v32, cum)

    t = lax.broadcasted_iota(jnp.int32, (C, C), 0)
    s = lax.broadcasted_iota(jnp.int32, (C, C), 1)
    mask_d = (t // 16 == s // 16) & (s <= t)
    masks = [(t // (2 * hf) == s // (2 * hf)) & ((t // hf) % 2 == 1) & ((s // hf) % 2 == 0)
             for hf in (16, 32, 64)]

    def head_stages(h):
        sl = slice(h * HEAD_W, (h + 1) * HEAD_W)
        q, k, g = _gates(rq_ref[:, sl].astype(F32), rf_ref[:, sl], lb_ref[:, sl])
        v32 = ri_ref[:, sl].astype(F32)
        cum = cumsum(g)
        yield
        st = st_scr[h]
        qe = (q * jnp.exp(cum)).astype(BF16)
        o = lax.dot_general(qe, st.astype(BF16), _NT, preferred_element_type=F32)
        yield

        def pair_scores(ref, clamp):
            dq = cum - ref
            dk = ref - cum
            if clamp:
                dq = jnp.minimum(dq, 0.0)
                dk = jnp.minimum(dk, 0.0)
            return lax.dot_general((q * jnp.exp(dq)).astype(BF16),
                                   (k * jnp.exp(dk)).astype(BF16), _NT,
                                   preferred_element_type=F32)

        a = jnp.where(mask_d, pair_scores(_block_rows(cum, 16, 7), False), 0.0)
        yield
        for hf, mk in zip((16, 32, 64), masks):
            a = jnp.where(mk, pair_scores(_block_rows(cum, 2 * hf, hf - 1), True), a)
            yield
        o = o + jnp.dot(a.astype(BF16), v32.astype(BF16), preferred_element_type=F32)
        st_scr[h] = state_update(st, k, v32, cum)
        yield
        rg = rg_ref[:, sl].astype(F32)
        o = _rms(o, nw_ref[:, sl]) * (rg * jax.nn.sigmoid(rg))
        o_ref[:, sl] = o.astype(o_ref.dtype)

    for _ in itertools.zip_longest(*[head_stages(h) for h in range(REC_HEADS)]):
        pass


def _hgrn(cm, lb, nw, rq, rf, ri, rg, mq, mf, mi):
    B, S, _ = rq.shape
    C = REC_CHUNK
    tok = pl.BlockSpec((None, C, GROUP_W), lambda b, c: (b, c, 0))
    fix = lambda shape: pl.BlockSpec(shape, lambda b, c: (0, 0))
    return pl.pallas_call(
        _hgrn_kernel,
        grid=(B, S // C),
        in_specs=[fix(cm.shape), fix((1, GROUP_W)), fix((1, GROUP_W)),
                  tok, tok, tok, tok,
                  fix((C, GROUP_W)), fix((C, GROUP_W)), fix((C, GROUP_W))],
        out_specs=tok,
        out_shape=jax.ShapeDtypeStruct((B, S, GROUP_W), BF16),
        scratch_shapes=[pltpu.VMEM((REC_HEADS, HEAD_W, HEAD_W), F32)],
        compiler_params=pltpu.CompilerParams(
            dimension_semantics=("arbitrary", "arbitrary"),
            vmem_limit_bytes=VMEM_LIMIT),
        name="hgrn2",
    )(cm, lb, nw, rq, rf, ri, rg, mq, mf, mi)


def _outproj_kernel(x_ref, att_ref, rec_ref, wo_ref, fw_ref, wq_ref, sk_ref,
                    h_ref, xt_ref, st_ref):
    h1 = (x_ref[...]
          + jnp.dot(att_ref[...], wo_ref[:GROUP_W, :], preferred_element_type=F32)
          + jnp.dot(rec_ref[...], wo_ref[GROUP_W:, :], preferred_element_type=F32))
    h_ref[...] = h1
    xt = _rms(h1, fw_ref[...]).T.astype(BF16)
    xt_ref[...] = xt
    qt = jnp.dot(wq_ref[...], xt, preferred_element_type=F32).astype(BF16)
    for g in range(2 * PEER_HEADS):
        sl = slice(g * N_KEYS, (g + 1) * N_KEYS)
        st_ref[sl, :] = jnp.dot(sk_ref[g], qt[sl, :], preferred_element_type=F32)


def _outproj(x2, att, rec, wo, fw, wqt, sk, tm):
    n = x2.shape[0]
    row = lambda i: (i, 0)
    colb = lambda i: (0, i)
    fixed = lambda i: (0, 0)
    return pl.pallas_call(
        _outproj_kernel,
        grid=(n // tm,),
        in_specs=[pl.BlockSpec((tm, D_MODEL), row),
                  pl.BlockSpec((tm, GROUP_W), row),
                  pl.BlockSpec((tm, GROUP_W), row),
                  pl.BlockSpec((D_MODEL, D_MODEL), fixed),
                  pl.BlockSpec((1, D_MODEL), fixed),
                  pl.BlockSpec(wqt.shape, fixed),
                  pl.BlockSpec(sk.shape, lambda i: (0, 0, 0))],
        out_specs=[pl.BlockSpec((tm, D_MODEL), row),
                   pl.BlockSpec((D_MODEL, tm), colb),
                   pl.BlockSpec((wqt.shape[0], tm), colb)],
        out_shape=[jax.ShapeDtypeStruct((n, D_MODEL), F32),
                   jax.ShapeDtypeStruct((D_MODEL, n), BF16),
                   jax.ShapeDtypeStruct((wqt.shape[0], n), F32)],
        compiler_params=pltpu.CompilerParams(
            dimension_semantics=("arbitrary",), vmem_limit_bytes=VMEM_LIMIT),
        name="outproj_scores",
    )(x2, att, rec, wo, fw, wqt, sk)


def _staircase():
    return [(p, q) for p in range(PEER_TOPK) for q in range(PEER_TOPK)
            if (p + 1) * (q + 1) <= PEER_TOPK]


def _sort_network(n):
    pairs = []
    p = 1
    while p < n:
        k = p
        while k >= 1:
            for j in range(k % p, n - k, 2 * k):
                for i in range(min(k, n - j - k)):
                    if (i + j) // (2 * p) == (i + j + k) // (2 * p):
                        pairs.append((i + j, i + j + k))
            k //= 2
        p *= 2
    return pairs


def _sorted_desc(xs):
    xs = list(xs)
    for i, j in _sort_network(len(xs)):
        xs[i], xs[j] = jnp.maximum(xs[i], xs[j]), jnp.minimum(xs[i], xs[j])
    return xs


def _merge_top(a, b):
    n = len(a)
    xs = [jnp.maximum(a[i], b[n - 1 - i]) for i in range(n)]
    d = n // 2
    while d >= 1:
        for i in range(n):
            if (i // d) % 2 == 0:
                xs[i], xs[i + d] = jnp.maximum(xs[i], xs[i + d]), jnp.minimum(xs[i], xs[i + d])
        d //= 2
    return xs


def _prefix_count(tops, test):
    assert len(tops) == 16

    def pick(bits, lo):
        step = 8
        idx = [lo]
        for _ in bits:
            idx = [i + d for i in idx for d in (0, step)]
            step //= 2
        vals = [tops[i] for i in idx]
        for b in reversed(bits):
            vals = [jnp.where(b, vals[2 * k + 1], vals[2 * k]) for k in range(len(vals) // 2)]
        return vals[0]

    t1 = test(tops[7])
    t2 = test(pick([t1], 3))
    t3 = test(pick([t1, t2], 1))
    t4 = test(pick([t1, t2, t3], 0))
    t5 = test(tops[15])
    one = lambda t, w: jnp.where(t, float(w), 0.0)
    return one(t1, 8) + one(t2, 4) + one(t3, 2) + one(t4, 1) + one(t5, 1)


def _route_kernel(st_ref, lam_ref, pw_ref, r2_ref, qw_ref, thr_scr, top_scr):
    K = PEER_TOPK
    SUB = F32_SUBLANES
    tb = st_ref.shape[1]
    pairs = _staircase()

    def key_slab(g, v):
        return st_ref[g * N_KEYS + v * SUB:g * N_KEYS + (v + 1) * SUB, :]

    for g in range(2 * PEER_HEADS):
        srt = _sorted_desc([key_slab(g, v) for v in range(N_KEYS // SUB)])
        for shift in (4, 2, 1):
            srt = _merge_top(srt, [pltpu.roll(x, shift, 0) for x in srt])
        for p in range(K):
            top_scr[g, p] = srt[p]

    class _Lazy:
        def __init__(self, g):
            self.g = g

        def __getitem__(self, p):
            return top_scr[self.g, p]

        def __len__(self):
            return K

    tops = [_Lazy(g) for g in range(2 * PEER_HEADS)]

    sub = lax.broadcasted_iota(jnp.int32, (PEER_HEADS, tb), 0)

    def stack_heads(part, p):
        out = tops[part][p]
        for h in range(1, PEER_HEADS):
            out = jnp.where(sub == h, tops[2 * h + part][p], out)
        return out

    a = [stack_heads(0, p) for p in range(K)]
    b = [stack_heads(1, p) for p in range(K)]
    sums = [a[p] + b[q] for (p, q) in pairs]
    n_full = len(sums) // K * K
    top = None
    for k in range(0, n_full, K):
        run = _sorted_desc(sums[k:k + K])
        top = run if top is None else _merge_top(top, run)
    for extra in sums[n_full:]:
        top = [jnp.maximum(top[0], extra)] + [
            jnp.maximum(top[i], jnp.minimum(top[i - 1], extra)) for i in range(1, K)]
    thr_scr[...] = top[K - 1]
    thr = thr_scr[...]
    m0 = sums[0]
    z = jnp.zeros((PEER_HEADS, tb), F32)
    for v in sums:
        z = z + jnp.where(v >= thr, jnp.exp(v - m0), 0.0)
    inv_z = 1.0 / z

    for h in range(PEER_HEADS):
        top1, top2 = tops[2 * h], tops[2 * h + 1]
        thr_h = jnp.broadcast_to(thr[h:h + 1, :], (SUB, tb))
        inv_z_h = jnp.broadcast_to(inv_z[h:h + 1, :], (SUB, tb))
        for v in range(N_KEYS // SUB):
            rows = slice(v * SUB, (v + 1) * SUB)
            s1 = key_slab(2 * h, v)
            lam_ref[h, rows, :] = _prefix_count(top2, lambda bq: s1 + bq >= thr_h)
            pw_ref[h, rows, :] = jnp.exp(s1 - top1[0]) * inv_z_h * GELU_FOLD
        for v in range(0, N_KEYS // SUB, 2):
            rows = slice(v * SUB, (v + 2) * SUB)
            s2 = jnp.concatenate([key_slab(2 * h + 1, v), key_slab(2 * h + 1, v + 1)], axis=0)
            top2_w = [jnp.concatenate([top2[p], top2[p]], axis=0) for p in range(K)]
            rank = _prefix_count(top2_w, lambda bp: s2 < bp)
            r2_ref[h, rows, :] = rank.astype(r2_ref.dtype)
            qw_ref[h, rows, :] = jnp.exp(s2 - top2_w[0]).astype(qw_ref.dtype)


def _route(st, tb):
    n = st.shape[1]
    spec = pl.BlockSpec((PEER_HEADS, N_KEYS, tb), lambda i: (0, 0, i))
    shp = lambda dt: jax.ShapeDtypeStruct((PEER_HEADS, N_KEYS, n), dt)
    return pl.pallas_call(
        _route_kernel,
        grid=(n // tb,),
        in_specs=[pl.BlockSpec((st.shape[0], tb), lambda i: (0, i))],
        out_specs=[spec, spec, spec, spec],
        out_shape=[shp(F32), shp(F32), shp(BF16), shp(BF16)],
        scratch_shapes=[pltpu.VMEM((PEER_HEADS, tb), F32),
                        pltpu.VMEM((2 * PEER_HEADS, PEER_TOPK, F32_SUBLANES, tb), F32)],
        compiler_params=pltpu.CompilerParams(
            dimension_semantics=("arbitrary",), vmem_limit_bytes=VMEM_LIMIT),
        name="peer_route",
    )(st)


def _peer_kernel(xt_ref, u_ref, vt_ref, lam_ref, pw_ref, r2_ref, qw_ref,
                 h_ref, fw_ref, o_ref, acc_scr, hid0_scr, hid1_scr, *, te, n_e):
    e = pl.program_id(1)
    tb = xt_ref.shape[1]
    n_grp = te // N_KEYS
    hid_bufs = (hid0_scr, hid1_scr)

    def row_bcast(row):
        tile = jnp.broadcast_to(row, (BF16_TILE_ROWS, row.shape[1])).astype(BF16)
        return jnp.concatenate([tile] * (N_KEYS // BF16_TILE_ROWS), axis=0)

    def hidden_pieces(par):
        hid_scr = hid_bufs[par]
        pieces = []
        tw = min(tb, PEER_TOKEN_STRIP)
        gpp = PEER_HIDDEN_ROWS // N_KEYS
        for i0, ct in itertools.product(range(0, n_grp, gpp), range(tb // tw)):
            def piece(i0=i0, ct=ct):
                cols = slice(ct * tw, (ct + 1) * tw)
                prow = slice(i0 * N_KEYS, (i0 + gpp) * N_KEYS)
                ht_all = jnp.dot(u_ref[prow, :], xt_ref[:, cols], preferred_element_type=F32)
                for g in range(gpp):
                    rows = slice((i0 + g) * N_KEYS, (i0 + g + 1) * N_KEYS)
                    i = e * n_grp + i0 + g
                    ht = ht_all[g * N_KEYS:(g + 1) * N_KEYS]
                    act = (ht * (1.0 + lax.erf(ht))).astype(BF16)
                    gate = None
                    for h in range(PEER_HEADS):
                        lam_b = row_bcast(lam_ref[h, pl.ds(i, 1), :][:, cols])
                        p_b = row_bcast(pw_ref[h, pl.ds(i, 1), :][:, cols])
                        sel = jnp.where(r2_ref[h, :, cols] < lam_b, qw_ref[h, :, cols],
                                        jnp.zeros_like(p_b))
                        gate = sel * p_b if gate is None else gate + sel * p_b
                    hid_scr[rows, cols] = act * gate
            pieces.append(piece)
        return pieces

    def value_pieces(par):
        hid_scr = hid_bufs[par]
        k_tiles = te // MXU_TILE
        pieces = []
        for nt in range(tb // MXU_TILE):
            cols = slice(nt * MXU_TILE, (nt + 1) * MXU_TILE)
            state = {}
            for kt in range(k_tiles):
                ks = slice(kt * MXU_TILE, (kt + 1) * MXU_TILE)

                def piece(ks=ks, cols=cols, kt=kt, state=state):
                    part = jnp.dot(vt_ref[:, ks], hid_scr[ks, cols], preferred_element_type=F32)
                    state["sum"] = part if kt == 0 else state["sum"] + part
                    if kt == k_tiles - 1:
                        acc_scr[:, cols] += state.pop("sum")
                pieces.append(piece)
        return pieces

    def run(*stages):
        order = sorted(((k + 0.5) / len(st), s_idx, k)
                       for s_idx, st in enumerate(stages) for k in range(len(st)))
        for _, s_idx, k in order:
            stages[s_idx][k]()

    @pl.when(e == 0)
    def _():
        acc_scr[...] = jnp.zeros(acc_scr.shape, F32)
        run(hidden_pieces(0))

    for par in (0, 1):
        @pl.when((e >= 1) & (e < n_e) & (e % 2 == par))
        def _(par=par):
            run(value_pieces(1 - par), hidden_pieces(par))

    @pl.when(e == n_e)
    def _():
        run(value_pieces(1 - n_e % 2))
        o_ref[...] = _rms(h_ref[...] + acc_scr[...].T, fw_ref[...])


def _peer(xt, u_bf, vt_bf, lam, pw, r2, qw, h1, fw, tb, te):
    n = xt.shape[1]
    n_e = N_EXPERTS // te
    assert n_e >= 2
    kernel = functools.partial(_peer_kernel, te=te, n_e=n_e)
    rt = pl.BlockSpec((PEER_HEADS, N_KEYS, tb), lambda t, e: (0, 0, t))
    return pl.pallas_call(
        kernel,
        grid=(n // tb, n_e + 1),
        in_specs=[pl.BlockSpec((D_MODEL, tb), lambda t, e: (0, t)),
                  pl.BlockSpec((te, D_MODEL), lambda t, e: (jnp.minimum(e, n_e - 1), 0)),
                  pl.BlockSpec((None, D_MODEL, te), lambda t, e: (jnp.maximum(e - 1, 0), 0, 0)),
                  rt, rt, rt, rt,
                  pl.BlockSpec((tb, D_MODEL), lambda t, e: (t, 0)),
                  pl.BlockSpec((1, D_MODEL), lambda t, e: (0, 0))],
        out_specs=pl.BlockSpec((tb, D_MODEL), lambda t, e: (t, 0)),
        out_shape=jax.ShapeDtypeStruct((n, D_MODEL), F32),
        scratch_shapes=[pltpu.VMEM((D_MODEL, tb), F32),
                        pltpu.VMEM((te, tb), BF16), pltpu.VMEM((te, tb), BF16)],
        compiler_params=pltpu.CompilerParams(
            dimension_semantics=("arbitrary", "arbitrary"),
            vmem_limit_bytes=VMEM_LIMIT),
        name="peer_experts",
    )(xt, u_bf, vt_bf, lam, pw, r2, qw, h1, fw)


def _rope_tables(T):
    d = ATT_HEAD_DIM
    inv_freq = ROPE_THETA ** (-jnp.arange(0, d, 2, dtype=F32) / d)
    ang = jnp.arange(T, dtype=F32)[:, None] * inv_freq[None, :]
    ang = jnp.concatenate([ang, ang], axis=-1)
    sign = jnp.where(jnp.arange(d) < d // 2, -1.0, 1.0).astype(F32)
    cos = jnp.tile(jnp.cos(ang), (1, 2))
    sin = jnp.tile(jnp.sin(ang) * sign[None, :], (1, 2))
    return cos, sin


def _pick(n, prefs):
    for p in prefs:
        if n % p == 0:
            return p
    raise ValueError(f"no supported tile for extent {n}")


def kernel(x, meta_tokens, mix_norm_w, w_in, rec_lb_logits, rec_norm_w, diff_lambda_q1, diff_lambda_k1, diff_lambda_q2, diff_lambda_k2, diff_subln_w, w_out, ffn_norm_w, peer_w_query, peer_subkeys, peer_u, peer_v, final_norm_w):
    B, S, D = x.shape
    assert D == D_MODEL and w_in.shape[0] == 1 and S % REC_CHUNK == 0
    n = B * S
    x2 = x.reshape(n, D)

    cos, sin = _rope_tables(N_META + S)
    w_bf = w_in[0].astype(BF16)
    nw = mix_norm_w[0].reshape(1, D)

    tm = _pick(S, (512, 256, 128))
    q, k, v, rq, rf, ri, rg = _inproj(x2, nw, w_bf, cos[N_META:], sin[N_META:], tm)
    mq_a, mk_a, mv_a, mrq, mrf, mri, _ = _inproj(
        meta_tokens.astype(F32), nw, w_bf, cos[:N_META], sin[:N_META], N_META)

    bq = _pick(S, (512, 256, 128))
    pad_m = lambda z: jnp.pad(z, ((0, HEAD_W - N_META), (0, 0)))
    lam4 = jnp.stack([diff_lambda_q1[0], diff_lambda_k1[0],
                      diff_lambda_q2[0], diff_lambda_k2[0]]).astype(F32)
    sw = diff_subln_w[0].reshape(1, HEAD_W)
    r3 = lambda z: z.reshape(B, S, GROUP_W)
    att = _attention(lam4, r3(q), r3(k), r3(v), pad_m(mk_a), pad_m(mv_a), sw, bq)

    lb = jax.nn.softmax(rec_lb_logits.astype(F32), axis=0)[0].reshape(1, GROUP_W)
    front = lambda z: jnp.pad(z, ((REC_CHUNK - N_META, 0), (0, 0)))
    rec = _hgrn(jnp.asarray(_cum_matrix(), BF16), lb, rec_norm_w[0].reshape(1, GROUP_W),
                r3(rq), r3(rf), r3(ri), r3(rg), front(mrq), front(mrf), front(mri))

    wqt = peer_w_query[0].T.astype(BF16)
    sk = peer_subkeys[0].reshape(2 * PEER_HEADS, N_KEYS, -1).astype(BF16)
    h1, xt, st = _outproj(x2, att.reshape(n, GROUP_W), rec.reshape(n, GROUP_W),
                          w_out[0].astype(BF16), ffn_norm_w[0].reshape(1, D), wqt, sk, tm)

    lam, pw, r2, qw = _route(st, _pick(n, (256, 128)))

    tb = _pick(n, (512, 256, 128))
    te = PEER_EXPERT_TILE
    vt_tiles = peer_v[0].astype(BF16).reshape(N_EXPERTS // te, te, D).transpose(0, 2, 1)
    u_scaled = (peer_u[0] * (2.0 ** -0.5)).astype(BF16)
    out = _peer(xt, u_scaled, vt_tiles, lam, pw, r2, qw,
                h1, final_norm_w.reshape(1, D), tb, te)
    return out.reshape(B, S, D)
```

```python
import functools
import itertools
import math

import numpy as np
import jax
import jax.numpy as jnp
from jax import lax
from jax.experimental import pallas as pl
from jax.experimental.pallas import tpu as pltpu

F32 = jnp.float32
BF16 = jnp.bfloat16

D_MODEL = 1024
N_META = 16
ATT_HEADS = 4
ATT_HEAD_DIM = 64
REC_HEADS = 4
HEAD_W = 128
GROUP_W = 512
N_GROUPS = 7
ROPE_THETA = 10000.0
PEER_HEADS = 8
N_KEYS = 128
N_EXPERTS = N_KEYS * N_KEYS
PEER_TOPK = 16
EPS = 1e-6
LAM_INIT = 0.8 - 0.6 * math.exp(-0.3 * 0)
NEG = -1e30
REC_CHUNK = 128
ATT_HEADS_PER_STEP = 4
PEER_EXPERT_TILE = 2048
PEER_TOKEN_STRIP = 512
GELU_FOLD = 0.5 * 2.0 ** 0.5
PEER_HIDDEN_ROWS = 128
MXU_TILE = 256
F32_SUBLANES = 8
BF16_TILE_ROWS = 16
VMEM_LIMIT = 56 * 1024 * 1024

_NT = (((1,), (1,)), ((), ()))


def _rms(x, w):
    return x * lax.rsqrt(jnp.mean(x * x, axis=-1, keepdims=True) + EPS) * w


def _inproj_kernel(x_ref, nw_ref, w_ref, cos_ref, sin_ref,
                   q_ref, k_ref, v_ref, rq_ref, rf_ref, ri_ref, rg_ref):
    a = _rms(x_ref[...], nw_ref[...]).astype(BF16)
    cos = cos_ref[...]
    sin = sin_ref[...]
    lane = lax.broadcasted_iota(jnp.int32, (1, HEAD_W), 1)
    lo = (lane % ATT_HEAD_DIM) < (ATT_HEAD_DIM // 2)

    def proj(c):
        return jnp.dot(a, w_ref[:, c * GROUP_W:(c + 1) * GROUP_W],
                       preferred_element_type=F32)

    def rope(y, scale):
        outs = []
        for h in range(ATT_HEADS):
            z = y[:, h * HEAD_W:(h + 1) * HEAD_W]
            rot = jnp.where(lo, pltpu.roll(z, HEAD_W - 32, 1), pltpu.roll(z, 32, 1))
            r = z * cos + rot * sin
            if scale is not None:
                r = r * scale
            outs.append(r)
        return jnp.concatenate(outs, axis=1)

    q_ref[...] = rope(proj(0), ATT_HEAD_DIM ** -0.5 * math.log2(math.e)).astype(q_ref.dtype)
    k_ref[...] = rope(proj(1), None).astype(k_ref.dtype)
    v_ref[...] = proj(2).astype(v_ref.dtype)
    rq_ref[...] = proj(3).astype(rq_ref.dtype)
    rf_ref[...] = proj(4).astype(rf_ref.dtype)
    ri_ref[...] = proj(5).astype(ri_ref.dtype)
    rg_ref[...] = proj(6).astype(rg_ref.dtype)


def _inproj(x2, nw, w_bf, cos_t, sin_t, tm):
    n = x2.shape[0]
    n_pos = cos_t.shape[0] // tm
    row = lambda i: (i, 0)
    fixed = lambda i: (0, 0)
    pos = lambda i: (i % n_pos, 0)
    dts = (BF16, BF16, BF16, BF16, F32, BF16, BF16)
    return pl.pallas_call(
        _inproj_kernel,
        grid=(n // tm,),
        in_specs=[pl.BlockSpec((tm, D_MODEL), row),
                  pl.BlockSpec((1, D_MODEL), fixed),
                  pl.BlockSpec((D_MODEL, N_GROUPS * GROUP_W), fixed),
                  pl.BlockSpec((tm, HEAD_W), pos),
                  pl.BlockSpec((tm, HEAD_W), pos)],
        out_specs=[pl.BlockSpec((tm, GROUP_W), row)] * N_GROUPS,
        out_shape=[jax.ShapeDtypeStruct((n, GROUP_W), dt) for dt in dts],
        compiler_params=pltpu.CompilerParams(
            dimension_semantics=("arbitrary",), vmem_limit_bytes=VMEM_LIMIT),
        name="inproj",
    )(x2, nw, w_bf, cos_t, sin_t)


def _attn_kernel(lam_ref, q_ref, k_ref, v_ref, km_ref, vm_ref, sw_ref, o_ref,
                 m_scr, acc_scr, s0_scr, s1_scr, *, bq, nh):
    qi = pl.program_id(2)
    heads = range(nh)
    sls = [slice(h * HEAD_W, (h + 1) * HEAD_W) for h in heads]
    lane = lax.broadcasted_iota(jnp.int32, (1, HEAD_W), 1)
    qq = []
    for h in heads:
        q = q_ref[:, sls[h]]
        zero = jnp.zeros_like(q)
        qq.append(jnp.concatenate([jnp.where(lane < ATT_HEAD_DIM, q, zero),
                                   jnp.where(lane >= ATT_HEAD_DIM, q, zero)], axis=0))

    m_scr[...] = jnp.full(m_scr.shape, NEG, F32)
    acc_scr[...] = jnp.zeros(acc_scr.shape, F32)

    def kv(ref, j, h):
        return ref[pl.ds(pl.multiple_of(j * bq, bq), bq), sls[h]]

    def scores(h, kb):
        return lax.dot_general(qq[h], kb, _NT, preferred_element_type=F32)

    def accumulate(h, s, vb, mask):
        if mask is not None:
            s = jnp.where(mask, s, NEG)
        m_prev = m_scr[h]
        m_new = jnp.maximum(m_prev, jnp.max(s, axis=1, keepdims=True))
        yield
        alpha = jnp.exp2(m_prev - m_new)
        p = jnp.exp2(s - jnp.concatenate([m_new] * (s.shape[1] // HEAD_W), axis=1))
        yield
        v_ext = jnp.concatenate([vb, jnp.ones_like(vb)], axis=1)
        acc_scr[h] = (jnp.concatenate([alpha, alpha], axis=1) * acc_scr[h]
                      + jnp.dot(p.astype(BF16), v_ext, preferred_element_type=F32))
        m_scr[h] = m_new

    def lockstep(gens):
        for _ in itertools.zip_longest(*gens):
            pass

    col_m = lax.broadcasted_iota(jnp.int32, (1, km_ref.shape[0]), 1)
    lockstep([accumulate(h, scores(h, km_ref[:, sls[h]]), vm_ref[:, sls[h]], col_m < N_META)
              for h in heads])

    for h in heads:
        s0_scr[h] = scores(h, kv(k_ref, 0, h))

    def body(jj, carry):
        j = 2 * jj
        for h in heads:
            s1_scr[h] = scores(h, kv(k_ref, j + 1, h))
        lockstep([accumulate(h, s0_scr[h], kv(v_ref, j, h), None) for h in heads])
        for h in heads:
            s0_scr[h] = scores(h, kv(k_ref, j + 2, h))
        lockstep([accumulate(h, s1_scr[h], kv(v_ref, j + 1, h), None) for h in heads])
        return carry

    lax.fori_loop(0, qi // 2, body, 0)

    row = lax.broadcasted_iota(jnp.int32, (2 * bq, bq), 0) % bq
    col = lax.broadcasted_iota(jnp.int32, (2 * bq, bq), 1)
    causal = col <= row

    @pl.when(qi % 2 == 0)
    def _():
        lockstep([accumulate(h, s0_scr[h], kv(v_ref, qi, h), causal) for h in heads])

    @pl.when(qi % 2 == 1)
    def _():
        for h in heads:
            s1_scr[h] = scores(h, kv(k_ref, qi, h))
        lockstep([accumulate(h, s0_scr[h], kv(v_ref, qi - 1, h), None) for h in heads])
        lockstep([accumulate(h, s1_scr[h], kv(v_ref, qi, h), causal) for h in heads])

    lv = lam_ref[...]
    lam = (jnp.exp(jnp.sum(lv[0:1] * lv[1:2], axis=1, keepdims=True))
           - jnp.exp(jnp.sum(lv[2:3] * lv[3:4], axis=1, keepdims=True)) + LAM_INIT)
    for h in heads:
        acc = acc_scr[h]
        num = acc[:, :HEAD_W]
        den = acc[:, HEAD_W:]
        o = num[:bq] / den[:bq] - lam * (num[bq:] / den[bq:])
        o_ref[:, sls[h]] = (_rms(o, sw_ref[...]) * (1.0 - LAM_INIT)).astype(o_ref.dtype)


def _attention(lam4, q, k, v, km, vm, sw, bq):
    B, S, _ = q.shape
    nh = ATT_HEADS_PER_STEP
    w = nh * HEAD_W
    kernel = functools.partial(_attn_kernel, bq=bq, nh=nh)
    return pl.pallas_call(
        kernel,
        grid=(B, ATT_HEADS // nh, S // bq),
        in_specs=[pl.BlockSpec((4, ATT_HEAD_DIM), lambda b, h, i: (0, 0)),
                  pl.BlockSpec((None, bq, w), lambda b, h, i: (b, i, h)),
                  pl.BlockSpec((None, S, w), lambda b, h, i: (b, 0, h)),
                  pl.BlockSpec((None, S, w), lambda b, h, i: (b, 0, h)),
                  pl.BlockSpec((km.shape[0], w), lambda b, h, i: (0, h)),
                  pl.BlockSpec((km.shape[0], w), lambda b, h, i: (0, h)),
                  pl.BlockSpec((1, HEAD_W), lambda b, h, i: (0, 0))],
        out_specs=pl.BlockSpec((None, bq, w), lambda b, h, i: (b, i, h)),
        out_shape=jax.ShapeDtypeStruct((B, S, GROUP_W), BF16),
        scratch_shapes=[pltpu.VMEM((nh, 2 * bq, HEAD_W), F32),
                        pltpu.VMEM((nh, 2 * bq, 2 * HEAD_W), F32),
                        pltpu.VMEM((nh, 2 * bq, bq), F32),
                        pltpu.VMEM((nh, 2 * bq, bq), F32)],
        compiler_params=pltpu.CompilerParams(
            dimension_semantics=("arbitrary", "arbitrary", "arbitrary"),
            vmem_limit_bytes=VMEM_LIMIT),
        name="diff_attention",
    )(lam4, q, k, v, km, vm, sw)


def _cum_matrix():
    r = np.arange(REC_CHUNK)
    return (r[None, :] <= r[:, None]).astype(np.float32)


def _block_rows(cum, period, offset):
    return jnp.concatenate(
        [jnp.broadcast_to(cum[b * period + offset:b * period + offset + 1], (period, cum.shape[1]))
         for b in range(cum.shape[0] // period)], axis=0)


def _gates(rq, rf, lb):
    q = rq * jax.nn.sigmoid(rq)
    f = lb + (1.0 - lb) * jax.nn.sigmoid(rf)
    return q, 1.0 - f, jnp.log(f)


def _hgrn_kernel(cm_ref, lb_ref, nw_ref, rq_ref, rf_ref, ri_ref, rg_ref,
                 mq_ref, mf_ref, mi_ref, o_ref, st_scr):
    c_idx = pl.program_id(1)
    C = REC_CHUNK
    tri = cm_ref[...]

    def cumsum(g):
        g0 = g.astype(BF16)
        r1 = g - g0.astype(F32)
        g1 = r1.astype(BF16)
        g2 = (r1 - g1.astype(F32)).astype(BF16)
        return (jnp.dot(tri, g0, preferred_element_type=F32)
                + jnp.dot(tri, g1, preferred_element_type=F32)
                + jnp.dot(tri, g2, preferred_element_type=F32))

    def state_update(st, k, v32, cum):
        last = cum[C - 1:C]
        ke = (k * jnp.exp(last - cum)).astype(BF16)
        upd = jnp.dot(v32.T.astype(BF16), ke, preferred_element_type=F32)
        return st * jnp.exp(last) + upd

    @pl.when(c_idx == 0)
    def _():
        valid = lax.broadcasted_iota(jnp.int32, (C, 1), 0) >= C - N_META
        for h in range(REC_HEADS):
            sl = slice(h * HEAD_W, (h + 1) * HEAD_W)
            _, k, g = _gates(mq_ref[:, sl].astype(F32), mf_ref[:, sl], lb_ref[:, sl])
            k = jnp.where(valid, k, 0.0)
            g = jnp.where(valid, g, 0.0)
            v32 = jnp.where(valid, mi_ref[:, sl].astype(F32), 0.0)
            cum = cumsum(g)
            st_scr[h] = state_update(jnp.zeros((HEAD_W, HEAD_W), F32), k, v32, cum)

    t = lax.broadcasted_iota(jnp.int32, (C, C), 0)
    s = lax.broadcasted_iota(jnp.int32, (C, C), 1)
    mask_d = (t // 16 == s // 16) & (s <= t)
    masks = [(t // (2 * hf) == s // (2 * hf)) & ((t // hf) % 2 == 1) & ((s // hf) % 2 == 0)
             for hf in (16, 32, 64)]

    def head_stages(h):
        sl = slice(h * HEAD_W, (h + 1) * HEAD_W)
        q, k, g = _gates(rq_ref[:, sl].astype(F32), rf_ref[:, sl], lb_ref[:, sl])
        v32 = ri_ref[:, sl].astype(F32)
        cum = cumsum(g)
        yield
        st = st_scr[h]
        qe = (q * jnp.exp(cum)).astype(BF16)
        o = lax.dot_general(qe, st.astype(BF16), _NT, preferred_element_type=F32)
        yield

        def pair_scores(ref, clamp):
            dq = cum - ref
            dk = ref - cum
            if clamp:
                dq = jnp.minimum(dq, 0.0)
                dk = jnp.minimum(dk, 0.0)
            return lax.dot_general((q * jnp.exp(dq)).astype(BF16),
                                   (k * jnp.exp(dk)).astype(BF16), _NT,
                                   preferred_element_type=F32)

        a = jnp.where(mask_d, pair_scores(_block_rows(cum, 16, 7), False), 0.0)
        yield
        for hf, mk in zip((16, 32, 64), masks):
            a = jnp.where(mk, pair_scores(_block_rows(cum, 2 * hf, hf - 1), True), a)
            yield
        o = o + jnp.dot(a.astype(BF16), v32.astype(BF16), preferred_element_type=F32)
        st_scr[h] = state_update(st, k, v32, cum)
        yield
        rg = rg_ref[:, sl].astype(F32)
        o = _rms(o, nw_ref[:, sl]) * (rg * jax.nn.sigmoid(rg))
        o_ref[:, sl] = o.astype(o_ref.dtype)

    for _ in itertools.zip_longest(*[head_stages(h) for h in range(REC_HEADS)]):
        pass


def _hgrn(cm, lb, nw, rq, rf, ri, rg, mq, mf, mi):
    B, S, _ = rq.shape
    C = REC_CHUNK
    tok = pl.BlockSpec((None, C, GROUP_W), lambda b, c: (b, c, 0))
    fix = lambda shape: pl.BlockSpec(shape, lambda b, c: (0, 0))
    return pl.pallas_call(
        _hgrn_kernel,
        grid=(B, S // C),
        in_specs=[fix(cm.shape), fix((1, GROUP_W)), fix((1, GROUP_W)),
                  tok, tok, tok, tok,
                  fix((C, GROUP_W)), fix((C, GROUP_W)), fix((C, GROUP_W))],
        out_specs=tok,
        out_shape=jax.ShapeDtypeStruct((B, S, GROUP_W), BF16),
        scratch_shapes=[pltpu.VMEM((REC_HEADS, HEAD_W, HEAD_W), F32)],
        compiler_params=pltpu.CompilerParams(
            dimension_semantics=("arbitrary", "arbitrary"),
            vmem_limit_bytes=VMEM_LIMIT),
        name="hgrn2",
    )(cm, lb, nw, rq, rf, ri, rg, mq, mf, mi)


def _outproj_kernel(x_ref, att_ref, rec_ref, wo_ref, fw_ref, wq_ref, sk_ref,
                    h_ref, xt_ref, st_ref):
    h1 = (x_ref[...]
          + jnp.dot(att_ref[...], wo_ref[:GROUP_W, :], preferred_element_type=F32)
          + jnp.dot(rec_ref[...], wo_ref[GROUP_W:, :], preferred_element_type=F32))
    h_ref[...] = h1
    xt = _rms(h1, fw_ref[...]).T.astype(BF16)
    xt_ref[...] = xt
    qt = jnp.dot(wq_ref[...], xt, preferred_element_type=F32).astype(BF16)
    for g in range(2 * PEER_HEADS):
        sl = slice(g * N_KEYS, (g + 1) * N_KEYS)
        st_ref[sl, :] = jnp.dot(sk_ref[g], qt[sl, :], preferred_element_type=F32)


def _outproj(x2, att, rec, wo, fw, wqt, sk, tm):
    n = x2.shape[0]
    row = lambda i: (i, 0)
    colb = lambda i: (0, i)
    fixed = lambda i: (0, 0)
    return pl.pallas_call(
        _outproj_kernel,
        grid=(n // tm,),
        in_specs=[pl.BlockSpec((tm, D_MODEL), row),
                  pl.BlockSpec((tm, GROUP_W), row),
                  pl.BlockSpec((tm, GROUP_W), row),
                  pl.BlockSpec((D_MODEL, D_MODEL), fixed),
                  pl.BlockSpec((1, D_MODEL), fixed),
                  pl.BlockSpec(wqt.shape, fixed),
                  pl.BlockSpec(sk.shape, lambda i: (0, 0, 0))],
        out_specs=[pl.BlockSpec((tm, D_MODEL), row),
                   pl.BlockSpec((D_MODEL, tm), colb),
                   pl.BlockSpec((wqt.shape[0], tm), colb)],
        out_shape=[jax.ShapeDtypeStruct((n, D_MODEL), F32),
                   jax.ShapeDtypeStruct((D_MODEL, n), BF16),
                   jax.ShapeDtypeStruct((wqt.shape[0], n), F32)],
        compiler_params=pltpu.CompilerParams(
            dimension_semantics=("arbitrary",), vmem_limit_bytes=VMEM_LIMIT),
        name="outproj_scores",
    )(x2, att, rec, wo, fw, wqt, sk)


def _staircase():
    return [(p, q) for p in range(PEER_TOPK) for q in range(PEER_TOPK)
            if (p + 1) * (q + 1) <= PEER_TOPK]


def _sort_network(n):
    pairs = []
    p = 1
    while p < n:
        k = p
        while k >= 1:
            for j in range(k % p, n - k, 2 * k):
                for i in range(min(k, n - j - k)):
                    if (i + j) // (2 * p) == (i + j + k) // (2 * p):
                        pairs.append((i + j, i + j + k))
            k //= 2
        p *= 2
    return pairs


def _sorted_desc(xs):
    xs = list(xs)
    for i, j in _sort_network(len(xs)):
        xs[i], xs[j] = jnp.maximum(xs[i], xs[j]), jnp.minimum(xs[i], xs[j])
    return xs


def _merge_top(a, b):
    n = len(a)
    xs = [jnp.maximum(a[i], b[n - 1 - i]) for i in range(n)]
    d = n // 2
    while d >= 1:
        for i in range(n):
            if (i // d) % 2 == 0:
                xs[i], xs[i + d] = jnp.maximum(xs[i], xs[i + d]), jnp.minimum(xs[i], xs[i + d])
        d //= 2
    return xs


def _prefix_count(tops, test):
    assert len(tops) == 16

    def pick(bits, lo):
        step = 8
        idx = [lo]
        for _ in bits:
            idx = [i + d for i in idx for d in (0, step)]
            step //= 2
        vals = [tops[i] for i in idx]
        for b in reversed(bits):
            vals = [jnp.where(b, vals[2 * k + 1], vals[2 * k]) for k in range(len(vals) // 2)]
        return vals[0]

    t1 = test(tops[7])
    t2 = test(pick([t1], 3))
    t3 = test(pick([t1, t2], 1))
    t4 = test(pick([t1, t2, t3], 0))
    t5 = test(tops[15])
    one = lambda t, w: jnp.where(t, float(w), 0.0)
    return one(t1, 8) + one(t2, 4) + one(t3, 2) + one(t4, 1) + one(t5, 1)


def _route_kernel(st_ref, lam_ref, pw_ref, r2_ref, qw_ref, thr_scr, top_scr):
    K = PEER_TOPK
    SUB = F32_SUBLANES
    tb = st_ref.shape[1]
    pairs = _staircase()

    def key_slab(g, v):
        return st_ref[g * N_KEYS + v * SUB:g * N_KEYS + (v + 1) * SUB, :]

    for g in range(2 * PEER_HEADS):
        srt = _sorted_desc([key_slab(g, v) for v in range(N_KEYS // SUB)])
        for shift in (4, 2, 1):
            srt = _merge_top(srt, [pltpu.roll(x, shift, 0) for x in srt])
        for p in range(K):
            top_scr[g, p] = srt[p]

    class _Lazy:
        def __init__(self, g):
            self.g = g

        def __getitem__(self, p):
            return top_scr[self.g, p]

        def __len__(self):
            return K

    tops = [_Lazy(g) for g in range(2 * PEER_HEADS)]

    sub = lax.broadcasted_iota(jnp.int32, (PEER_HEADS, tb), 0)

    def stack_heads(part, p):
        out = tops[part][p]
        for h in range(1, PEER_HEADS):
            out = jnp.where(sub == h, tops[2 * h + part][p], out)
        return out

    a = [stack_heads(0, p) for p in range(K)]
    b = [stack_heads(1, p) for p in range(K)]
    sums = [a[p] + b[q] for (p, q) in pairs]
    n_full = len(sums) // K * K
    top = None
    for k in range(0, n_full, K):
        run = _sorted_desc(sums[k:k + K])
        top = run if top is None else _merge_top(top, run)
    for extra in sums[n_full:]:
        top = [jnp.maximum(top[0], extra)] + [
            jnp.maximum(top[i], jnp.minimum(top[i - 1], extra)) for i in range(1, K)]
    thr_scr[...] = top[K - 1]
    thr = thr_scr[...]
    m0 = sums[0]
    z = jnp.zeros((PEER_HEADS, tb), F32)
    for v in sums:
        z = z + jnp.where(v >= thr, jnp.exp(v - m0), 0.0)
    inv_z = 1.0 / z

    for h in range(PEER_HEADS):
        top1, top2 = tops[2 * h], tops[2 * h + 1]
        thr_h = jnp.broadcast_to(thr[h:h + 1, :], (SUB, tb))
        inv_z_h = jnp.broadcast_to(inv_z[h:h + 1, :], (SUB, tb))
        for v in range(N_KEYS // SUB):
            rows = slice(v * SUB, (v + 1) * SUB)
            s1 = key_slab(2 * h, v)
            lam_ref[h, rows, :] = _prefix_count(top2, lambda bq: s1 + bq >= thr_h)
            pw_ref[h, rows, :] = jnp.exp(s1 - top1[0]) * inv_z_h * GELU_FOLD
        for v in range(0, N_KEYS // SUB, 2):
            rows = slice(v * SUB, (v + 2) * SUB)
            s2 = jnp.concatenate([key_slab(2 * h + 1, v), key_slab(2 * h + 1, v + 1)], axis=0)
            top2_w = [jnp.concatenate([top2[p], top2[p]], axis=0) for p in range(K)]
            rank = _prefix_count(top2_w, lambda bp: s2 < bp)
            r2_ref[h, rows, :] = rank.astype(r2_ref.dtype)
            qw_ref[h, rows, :] = jnp.exp(s2 - top2_w[0]).astype(qw_ref.dtype)


def _route(st, tb):
    n = st.shape[1]
    spec = pl.BlockSpec((PEER_HEADS, N_KEYS, tb), lambda i: (0, 0, i))
    shp = lambda dt: jax.ShapeDtypeStruct((PEER_HEADS, N_KEYS, n), dt)
    return pl.pallas_call(
        _route_kernel,
        grid=(n // tb,),
        in_specs=[pl.BlockSpec((st.shape[0], tb), lambda i: (0, i))],
        out_specs=[spec, spec, spec, spec],
        out_shape=[shp(F32), shp(F32), shp(BF16), shp(BF16)],
        scratch_shapes=[pltpu.VMEM((PEER_HEADS, tb), F32),
                        pltpu.VMEM((2 * PEER_HEADS, PEER_TOPK, F32_SUBLANES, tb), F32)],
        compiler_params=pltpu.CompilerParams(
            dimension_semantics=("arbitrary",), vmem_limit_bytes=VMEM_LIMIT),
        name="peer_route",
    )(st)


def _peer_kernel(xt_ref, u_ref, vt_ref, lam_ref, pw_ref, r2_ref, qw_ref,
                 h_ref, fw_ref, o_ref, acc_scr, hid0_scr, hid1_scr, *, te, n_e):
    e = pl.program_id(1)
    tb = xt_ref.shape[1]
    n_grp = te // N_KEYS
    hid_bufs = (hid0_scr, hid1_scr)

    def row_bcast(row):
        tile = jnp.broadcast_to(row, (BF16_TILE_ROWS, row.shape[1])).astype(BF16)
        return jnp.concatenate([tile] * (N_KEYS // BF16_TILE_ROWS), axis=0)

    def hidden_pieces(par):
        hid_scr = hid_bufs[par]
        pieces = []
        tw = min(tb, PEER_TOKEN_STRIP)
        gpp = PEER_HIDDEN_ROWS // N_KEYS
        for i0, ct in itertools.product(range(0, n_grp, gpp), range(tb // tw)):
            def piece(i0=i0, ct=ct):
                cols = slice(ct * tw, (ct + 1) * tw)
                prow = slice(i0 * N_KEYS, (i0 + gpp) * N_KEYS)
                ht_all = jnp.dot(u_ref[prow, :], xt_ref[:, cols], preferred_element_type=F32)
                for g in range(gpp):
                    rows = slice((i0 + g) * N_KEYS, (i0 + g + 1) * N_KEYS)
                    i = e * n_grp + i0 + g
                    ht = ht_all[g * N_KEYS:(g + 1) * N_KEYS]
                    act = (ht * (1.0 + lax.erf(ht))).astype(BF16)
                    gate = None
                    for h in range(PEER_HEADS):
                        lam_b = row_bcast(lam_ref[h, pl.ds(i, 1), :][:, cols])
                        p_b = row_bcast(pw_ref[h, pl.ds(i, 1), :][:, cols])
                        sel = jnp.where(r2_ref[h, :, cols] < lam_b, qw_ref[h, :, cols],
                                        jnp.zeros_like(p_b))
                        gate = sel * p_b if gate is None else gate + sel * p_b
                    hid_scr[rows, cols] = act * gate
            pieces.append(piece)
        return pieces

    def value_pieces(par):
        hid_scr = hid_bufs[par]
        k_tiles = te // MXU_TILE
        pieces = []
        for nt in range(tb // MXU_TILE):
            cols = slice(nt * MXU_TILE, (nt + 1) * MXU_TILE)
            state = {}
            for kt in range(k_tiles):
                ks = slice(kt * MXU_TILE, (kt + 1) * MXU_TILE)

                def piece(ks=ks, cols=cols, kt=kt, state=state):
                    part = jnp.dot(vt_ref[:, ks], hid_scr[ks, cols], preferred_element_type=F32)
                    state["sum"] = part if kt == 0 else state["sum"] + part
                    if kt == k_tiles - 1:
                        acc_scr[:, cols] += state.pop("sum")
                pieces.append(piece)
        return pieces

    def run(*stages):
        order = sorted(((k + 0.5) / len(st), s_idx, k)
                       for s_idx, st in enumerate(stages) for k in range(len(st)))
        for _, s_idx, k in order:
            stages[s_idx][k]()

    @pl.when(e == 0)
    def _():
        acc_scr[...] = jnp.zeros(acc_scr.shape, F32)
        run(hidden_pieces(0))

    for par in (0, 1):
        @pl.when((e >= 1) & (e < n_e) & (e % 2 == par))
        def _(par=par):
            run(value_pieces(1 - par), hidden_pieces(par))

    @pl.when(e == n_e)
    def _():
        run(value_pieces(1 - n_e % 2))
        o_ref[...] = _rms(h_ref[...] + acc_scr[...].T, fw_ref[...])


def _peer(xt, u_bf, vt_bf, lam, pw, r2, qw, h1, fw, tb, te):
    n = xt.shape[1]
    n_e = N_EXPERTS // te
    assert n_e >= 2
    kernel = functools.partial(_peer_kernel, te=te, n_e=n_e)
    rt = pl.BlockSpec((PEER_HEADS, N_KEYS, tb), lambda t, e: (0, 0, t))
    return pl.pallas_call(
        kernel,
        grid=(n // tb, n_e + 1),
        in_specs=[pl.BlockSpec((D_MODEL, tb), lambda t, e: (0, t)),
                  pl.BlockSpec((te, D_MODEL), lambda t, e: (jnp.minimum(e, n_e - 1), 0)),
                  pl.BlockSpec((None, D_MODEL, te), lambda t, e: (jnp.maximum(e - 1, 0), 0, 0)),
                  rt, rt, rt, rt,
                  pl.BlockSpec((tb, D_MODEL), lambda t, e: (t, 0)),
                  pl.BlockSpec((1, D_MODEL), lambda t, e: (0, 0))],
        out_specs=pl.BlockSpec((tb, D_MODEL), lambda t, e: (t, 0)),
        out_shape=jax.ShapeDtypeStruct((n, D_MODEL), F32),
        scratch_shapes=[pltpu.VMEM((D_MODEL, tb), F32),
                        pltpu.VMEM((te, tb), BF16), pltpu.VMEM((te, tb), BF16)],
        compiler_params=pltpu.CompilerParams(
            dimension_semantics=("arbitrary", "arbitrary"),
            vmem_limit_bytes=VMEM_LIMIT),
        name="peer_experts",
    )(xt, u_bf, vt_bf, lam, pw, r2, qw, h1, fw)


def _rope_tables(T):
    d = ATT_HEAD_DIM
    inv_freq = ROPE_THETA ** (-jnp.arange(0, d, 2, dtype=F32) / d)
    ang = jnp.arange(T, dtype=F32)[:, None] * inv_freq[None, :]
    ang = jnp.concatenate([ang, ang], axis=-1)
    sign = jnp.where(jnp.arange(d) < d // 2, -1.0, 1.0).astype(F32)
    cos = jnp.tile(jnp.cos(ang), (1, 2))
    sin = jnp.tile(jnp.sin(ang) * sign[None, :], (1, 2))
    return cos, sin


def _pick(n, prefs):
    for p in prefs:
        if n % p == 0:
            return p
    raise ValueError(f"no supported tile for extent {n}")


def kernel(x, meta_tokens, mix_norm_w, w_in, rec_lb_logits, rec_norm_w, diff_lambda_q1, diff_lambda_k1, diff_lambda_q2, diff_lambda_k2, diff_subln_w, w_out, ffn_norm_w, peer_w_query, peer_subkeys, peer_u, peer_v, final_norm_w):
    B, S, D = x.shape
    assert D == D_MODEL and w_in.shape[0] == 1 and S % REC_CHUNK == 0
    n = B * S
    x2 = x.reshape(n, D)

    cos, sin = _rope_tables(N_META + S)
    w_bf = w_in[0].astype(BF16)
    nw = mix_norm_w[0].reshape(1, D)

    tm = _pick(S, (512, 256, 128))
    q, k, v, rq, rf, ri, rg = _inproj(x2, nw, w_bf, cos[N_META:], sin[N_META:], tm)
    mq_a, mk_a, mv_a, mrq, mrf, mri, _ = _inproj(
        meta_tokens.astype(F32), nw, w_bf, cos[:N_META], sin[:N_META], N_META)

    bq = _pick(S, (512, 256, 128))
    pad_m = lambda z: jnp.pad(z, ((0, HEAD_W - N_META), (0, 0)))
    lam4 = jnp.stack([diff_lambda_q1[0], diff_lambda_k1[0],
                      diff_lambda_q2[0], diff_lambda_k2[0]]).astype(F32)
    sw = diff_subln_w[0].reshape(1, HEAD_W)
    r3 = lambda z: z.reshape(B, S, GROUP_W)
    att = _attention(lam4, r3(q), r3(k), r3(v), pad_m(mk_a), pad_m(mv_a), sw, bq)

    lb = jax.nn.softmax(rec_lb_logits.astype(F32), axis=0)[0].reshape(1, GROUP_W)
    front = lambda z: jnp.pad(z, ((REC_CHUNK - N_META, 0), (0, 0)))
    rec = _hgrn(jnp.asarray(_cum_matrix(), BF16), lb, rec_norm_w[0].reshape(1, GROUP_W),
                r3(rq), r3(rf), r3(ri), r3(rg), front(mrq), front(mrf), front(mri))

    wqt = peer_w_query[0].T.astype(BF16)
    sk = peer_subkeys[0].reshape(2 * PEER_HEADS, N_KEYS, -1).astype(BF16)
    h1, xt, st = _outproj(x2, att.reshape(n, GROUP_W), rec.reshape(n, GROUP_W),
                          w_out[0].astype(BF16), ffn_norm_w[0].reshape(1, D), wqt, sk, tm)

    lam, pw, r2, qw = _route(st, _pick(n, (256, 128)))

    tb = _pick(n, (512, 256, 128))
    te = PEER_EXPERT_TILE
    vt_tiles = peer_v[0].astype(BF16).reshape(N_EXPERTS // te, te, D).transpose(0, 2, 1)
    u_scaled = (peer_u[0] * (2.0 ** -0.5)).astype(BF16)
    out = _peer(xt, u_scaled, vt_tiles, lam, pw, r2, qw,
                h1, final_norm_w.reshape(1, D), tb, te)
    return out.reshape(B, S, D)
```
